```python
import jax
import jax.numpy as jnp
from jax import lax
import numpy as np

D_MODEL = 4096
BATCH = 2
SEQ = 4096
DEPTH = 2

EPS = 1e-6
PLE_DIM = 256
CHUNK = 64
CHUNK_LOG2 = 6

GDN_HEADS = 16
GDN_DK = 128
GDN_DV = 128
GDN_QK = GDN_HEADS * GDN_DK
GDN_V = GDN_HEADS * GDN_DV
CONV_W = 4

GLA_HEADS = 16
GLA_DK = 64
GLA_DV = 128
GLA_QK = GLA_HEADS * GLA_DK
GLA_V = GLA_HEADS * GLA_DV
GLA_RANK = 16
GLA_TAU = 16.0

PEER_HEADS = 8
N_KEYS = 128
N_EXPERTS = N_KEYS * N_KEYS
D_KEY = 256
HALF_KEY = D_KEY // 2
TOPK_HALF = 16
TOPK = 16
PEER_BLOCK = 64
PEER_V_SCALE = 0.2

IN_SPLITS = (2 * GDN_QK + GDN_V, GDN_V, GDN_HEADS, GDN_HEADS, GLA_QK, GLA_QK, GLA_V, GLA_V, GLA_RANK, 2 * D_MODEL)
IN_WIDTH = sum(IN_SPLITS)

kernel_name = 'hybrid_gdn_gla_peer_ple'


def rms_norm(x, w):
    xf = x.astype(jnp.float32)
    y = xf * lax.rsqrt(jnp.mean(xf * xf, axis=-1, keepdims=True) + EPS)
    return (y * w.astype(jnp.float32)).astype(x.dtype)


def l2_norm(x):
    xf = x.astype(jnp.float32)
    return xf * lax.rsqrt(jnp.sum(xf * xf, axis=-1, keepdims=True) + EPS)


def causal_conv_silu(x, w):
    out = lax.conv_general_dilated(
        x, w[:, None, :].astype(x.dtype), window_strides=(1,), padding=[(CONV_W - 1, 0)],
        dimension_numbers=('NWC', 'WIO', 'NWC'), feature_group_count=x.shape[-1])
    return jax.nn.silu(out)


def _chunk(t):
    b, s, h, d = t.shape
    return t.reshape(b, s // CHUNK, CHUNK, h, d).transpose(1, 0, 3, 2, 4)


def _unchunk(t):
    n, b, h, c, d = t.shape
    return t.transpose(1, 0, 3, 2, 4).reshape(b, n * c, h, d)


def gated_delta_rule(q, k, v, log_decay, beta):
    q = q * GDN_DK ** -0.5
    qc, kc, vc = _chunk(q), _chunk(k), _chunk(v)
    gc = jnp.cumsum(_chunk(log_decay[..., None])[..., 0], axis=-1)
    bc = _chunk(beta[..., None])
    causal = jnp.tril(jnp.ones((CHUNK, CHUNK), dtype=bool))
    strict = jnp.tril(jnp.ones((CHUNK, CHUNK), dtype=bool), -1)
    decay = jnp.exp(jnp.where(causal, gc[..., :, None] - gc[..., None, :], -jnp.inf))
    k_beta = kc * bc
    n_mat = -jnp.where(strict, jnp.einsum('nbhid,nbhjd->nbhij', k_beta, kc) * decay, 0.0)
    t_inv = jnp.eye(CHUNK, dtype=q.dtype) + n_mat
    power = n_mat
    for _ in range(CHUNK_LOG2 - 1):
        power = power @ power
        t_inv = t_inv + t_inv @ power
    u = t_inv @ (vc * bc)
    w = t_inv @ (k_beta * jnp.exp(gc)[..., None])
    attn = jnp.einsum('nbhid,nbhjd->nbhij', qc, kc) * decay

    def step(state, inp):
        q_i, k_i, u_i, w_i, g_i, a_i = inp
        v_new = u_i - w_i @ state
        o_i = (q_i * jnp.exp(g_i)[..., None]) @ state + a_i @ v_new
        g_last = g_i[..., -1:]
        state = state * jnp.exp(g_last)[..., None] + jnp.einsum(
            'bhcd,bhce->bhde', k_i * jnp.exp(g_last - g_i)[..., None], v_new)
        return state, o_i

    s0 = jnp.zeros((q.shape[0], q.shape[2], GDN_DK, GDN_DV), q.dtype)
    _, o = lax.scan(step, s0, (qc, kc, u, w, gc, attn))
    return _unchunk(o)


def gla_chunked(q, k, v, log_a):
    q = q * GLA_DK ** -0.5
    qc, kc, vc = _chunk(q), _chunk(k), _chunk(v)
    bc = jnp.cumsum(_chunk(log_a), axis=-2)
    causal = jnp.tril(jnp.ones((CHUNK, CHUNK), dtype=bool))

    def step(state, inp):
        q_i, k_i, v_i, b_i = inp
        diff = b_i[..., :, None, :] - b_i[..., None, :, :]
        dec = jnp.exp(jnp.where(causal[:, :, None], diff, -jnp.inf))
        attn = jnp.einsum('bhid,bhjd,bhijd->bhij', q_i, k_i, dec)
        o_i = (q_i * jnp.exp(b_i)) @ state + attn @ v_i
        b_last = b_i[..., -1:, :]
        state = state * jnp.exp(b_last)[..., 0, :, None] + jnp.einsum(
            'bhcd,bhce->bhde', k_i * jnp.exp(b_last - b_i), v_i)
        return state, o_i

    s0 = jnp.zeros((q.shape[0], q.shape[2], GLA_DK, GLA_DV), q.dtype)
    _, o = lax.scan(step, s0, (qc, kc, vc, bc))
    return _unchunk(o)


def peer_ffn(h, w_query, keys1, keys2, expert_u, expert_v):
    b, s, d = h.shape
    n_tok = b * s
    tok = h.reshape(n_tok, d)
    q = (tok @ w_query).reshape(n_tok, PEER_HEADS, 2, HALF_KEY)
    s1 = jnp.einsum('thd,kd->thk', q[:, :, 0], keys1)
    s2 = jnp.einsum('thd,kd->thk', q[:, :, 1], keys2)
    v1, i1 = lax.top_k(s1, TOPK_HALF)
    v2, i2 = lax.top_k(s2, TOPK_HALF)
    cand_s = (v1[..., :, None] + v2[..., None, :]).reshape(n_tok, PEER_HEADS, TOPK_HALF * TOPK_HALF)
    cand_i = (i1[..., :, None] * N_KEYS + i2[..., None, :]).reshape(n_tok, PEER_HEADS, TOPK_HALF * TOPK_HALF)
    top_s, pos = lax.top_k(cand_s, TOPK)
    idx = jnp.take_along_axis(cand_i, pos, axis=-1)
    gate = jax.nn.softmax(top_s.astype(jnp.float32), axis=-1).astype(h.dtype)
    nb = n_tok // PEER_BLOCK

    def expert_block(args):
        x_blk, idx_blk, g_blk = args
        u = expert_u[idx_blk]
        act = jax.nn.gelu(jnp.einsum('td,thkd->thk', x_blk, u))
        v = expert_v[idx_blk]
        return jnp.einsum('thk,thkd->td', g_blk * act, v)

    out = lax.map(expert_block, (tok.reshape(nb, PEER_BLOCK, d),
                                 idx.reshape(nb, PEER_BLOCK, PEER_HEADS, TOPK),
                                 gate.reshape(nb, PEER_BLOCK, PEER_HEADS, TOPK)))
    return out.reshape(b, s, d)


def hybrid_layer(x, p_i, norm_mix, w_in, gdn_conv, gdn_a_log, gdn_dt_bias, gdn_norm,
                 gla_w_a2, gla_b_a, gla_norm, w_o_gdn, w_o_gla, w_out, norm_ffn,
                 peer_w_query, peer_keys1, peer_keys2, peer_u, peer_v,
                 norm_ple, w_ple, w_ple_gate):
    b, s, _ = x.shape
    f32 = jnp.float32
    h = rms_norm(x, norm_mix)
    split_points = np.cumsum(IN_SPLITS)[:-1].tolist()
    (g_qkv, g_z, g_beta, g_dt, l_q, l_k, l_v, l_r, l_a1, gate_logits) = jnp.split(h @ w_in, split_points, axis=-1)

    qkv = causal_conv_silu(g_qkv, gdn_conv)
    q, k, v = jnp.split(qkv, [GDN_QK, 2 * GDN_QK], axis=-1)
    q = l2_norm(q.reshape(b, s, GDN_HEADS, GDN_DK))
    k = l2_norm(k.reshape(b, s, GDN_HEADS, GDN_DK))
    v = v.reshape(b, s, GDN_HEADS, GDN_DV).astype(f32)
    beta = jax.nn.sigmoid(g_beta.astype(f32))
    log_decay = -jnp.exp(gdn_a_log.astype(f32)) * jax.nn.softplus(g_dt.astype(f32) + gdn_dt_bias.astype(f32))
    o_gdn = gated_delta_rule(q, k, v, log_decay, beta).astype(x.dtype)
    o_gdn = rms_norm(o_gdn, gdn_norm) * jax.nn.silu(g_z.reshape(b, s, GDN_HEADS, GDN_DV))
    y_gdn = o_gdn.reshape(b, s, GDN_V) @ w_o_gdn

    log_a = jax.nn.log_sigmoid((l_a1 @ gla_w_a2 + gla_b_a).astype(f32)) / GLA_TAU
    o_gla = gla_chunked(l_q.reshape(b, s, GLA_HEADS, GLA_DK).astype(f32),
                        l_k.reshape(b, s, GLA_HEADS, GLA_DK).astype(f32),
                        l_v.reshape(b, s, GLA_HEADS, GLA_DV).astype(f32),
                        log_a.reshape(b, s, GLA_HEADS, GLA_DK)).astype(x.dtype)
    o_gla = rms_norm(o_gla, gla_norm) * jax.nn.silu(l_r.reshape(b, s, GLA_HEADS, GLA_DV))
    y_gla = o_gla.reshape(b, s, GLA_V) @ w_o_gla

    gate_gdn, gate_gla = jnp.split(jax.nn.sigmoid(gate_logits), 2, axis=-1)
    x = x + (gate_gdn * y_gdn + gate_gla * y_gla) @ w_out

    x = x + peer_ffn(rms_norm(x, norm_ffn), peer_w_query, peer_keys1, peer_keys2, peer_u, peer_v)

    x = x + jax.nn.sigmoid(rms_norm(x, norm_ple) @ w_ple_gate) * (p_i @ w_ple)
    return x


def setup_inputs(seed: int = 0) -> dict:
    key = jax.random.key(seed)
    ks = jax.random.split(key, 32)

    def nrm(k, shape, scale):
        return jax.random.normal(k, shape, jnp.float32) * scale

    def gain(k, shape):
        return 1.0 + 0.01 * jax.random.normal(k, shape, jnp.float32)

    dt = jnp.exp(jax.random.uniform(ks[6], (DEPTH, GDN_HEADS), jnp.float32, np.log(1e-3), np.log(1e-1)))
    return {
        'x': nrm(ks[0], (BATCH, SEQ, D_MODEL), 1.0),
        'p': nrm(ks[1], (DEPTH, BATCH, SEQ, PLE_DIM), 1.0),
        'norm_mix': gain(ks[2], (DEPTH, D_MODEL)),
        'w_in': nrm(ks[3], (DEPTH, D_MODEL, IN_WIDTH), D_MODEL ** -0.5),
        'gdn_conv': nrm(ks[4], (DEPTH, CONV_W, 2 * GDN_QK + GDN_V), CONV_W ** -0.5),
        'gdn_a_log': jnp.log(jax.random.uniform(ks[5], (DEPTH, GDN_HEADS), jnp.float32, 1.0, 16.0)),
        'gdn_dt_bias': dt + jnp.log(-jnp.expm1(-dt)),
        'gdn_norm': gain(ks[7], (DEPTH, GDN_DV)),
        'gla_w_a2': nrm(ks[8], (DEPTH, GLA_RANK, GLA_QK), GLA_RANK ** -0.5),
        'gla_b_a': nrm(ks[9], (DEPTH, GLA_QK), 0.1),
        'gla_norm': gain(ks[10], (DEPTH, GLA_DV)),
        'w_o_gdn': nrm(ks[11], (DEPTH, GDN_V, D_MODEL), GDN_V ** -0.5),
        'w_o_gla': nrm(ks[12], (DEPTH, GLA_V, D_MODEL), GLA_V ** -0.5),
        'w_out': nrm(ks[13], (DEPTH, D_MODEL, D_MODEL), D_MODEL ** -0.5),
        'norm_ffn': gain(ks[14], (DEPTH, D_MODEL)),
        'peer_w_query': nrm(ks[15], (DEPTH, D_MODEL, PEER_HEADS * D_KEY), D_MODEL ** -0.5),
        'peer_keys1': nrm(ks[16], (DEPTH, N_KEYS, HALF_KEY), HALF_KEY ** -0.5),
        'peer_keys2': nrm(ks[17], (DEPTH, N_KEYS, HALF_KEY), HALF_KEY ** -0.5),
        'peer_u': nrm(ks[18], (DEPTH, N_EXPERTS, D_MODEL), D_MODEL ** -0.5),
        'peer_v': nrm(ks[19], (DEPTH, N_EXPERTS, D_MODEL), PEER_V_SCALE),
        'norm_ple': gain(ks[20], (DEPTH, D_MODEL)),
        'w_ple': nrm(ks[21], (DEPTH, PLE_DIM, D_MODEL), PLE_DIM ** -0.5),
        'w_ple_gate': nrm(ks[22], (DEPTH, D_MODEL, D_MODEL), D_MODEL ** -0.5),
        'norm_final': gain(ks[23], (D_MODEL,)),
    }


def reference(x, p, norm_mix, w_in, gdn_conv, gdn_a_log, gdn_dt_bias, gdn_norm,
              gla_w_a2, gla_b_a, gla_norm, w_o_gdn, w_o_gla, w_out, norm_ffn,
              peer_w_query, peer_keys1, peer_keys2, peer_u, peer_v,
              norm_ple, w_ple, w_ple_gate, norm_final):
    for i in range(DEPTH):
        x = hybrid_layer(x, p[i], norm_mix[i], w_in[i], gdn_conv[i], gdn_a_log[i], gdn_dt_bias[i],
                         gdn_norm[i], gla_w_a2[i], gla_b_a[i], gla_norm[i], w_o_gdn[i], w_o_gla[i],
                         w_out[i], norm_ffn[i], peer_w_query[i], peer_keys1[i], peer_keys2[i],
                         peer_u[i], peer_v[i], norm_ple[i], w_ple[i], w_ple_gate[i])
    return rms_norm(x, norm_final)
```

```python
import functools
import math

import jax
import jax.numpy as jnp
from jax import lax
from jax.experimental import pallas as pl
from jax.experimental.pallas import tpu as pltpu

EPS = 1e-6
CHUNK_GDN = 64
CHUNK_GLA = 64
SUB_GLA = 16
CONV_W = 4
GDN_HEADS = 16
GDN_D = 128
GLA_HEADS = 16
GLA_DK = 64
GLA_DV = 128
GLA_RANK = 16
GLA_TAU = 16.0
PEER_HEADS = 8
N_KEYS = 128
TOPK = 16
LANES = 128
VMEM_LIMIT = 56 * 1024 * 1024

F32 = jnp.float32
BF16 = jnp.bfloat16
HI = lax.Precision.HIGHEST
NT = (((1,), (1,)), ((), ()))
TN = (((0,), (0,)), ((), ()))


def _cparams(sem):
    return pltpu.CompilerParams(dimension_semantics=sem, vmem_limit_bytes=VMEM_LIMIT)


def _dot(a, b, dims=None, precision=None):
    if dims is None:
        return jnp.dot(a, b, preferred_element_type=F32, precision=precision)
    return lax.dot_general(a, b, dims, preferred_element_type=F32, precision=precision)


def _bdot(a, b, dims=None):
    return _dot(a.astype(BF16), b.astype(BF16), dims)


def _sigmoid(x):
    return 1.0 / (1.0 + jnp.exp(-x))


def _silu(x):
    return x * _sigmoid(x)


def _softplus(x):
    return jnp.maximum(x, 0.0) + jnp.log(1.0 + jnp.exp(-jnp.abs(x)))


def _gelu_tanh(x):
    return 0.5 * x * (1.0 + jnp.tanh(math.sqrt(2.0 / math.pi) * (x + 0.044715 * (x * x * x))))


def _rms_body(x, w):
    return x * lax.rsqrt(jnp.mean(x * x, axis=-1, keepdims=True) + EPS) * w


def _rmsnorm_kernel(x_ref, w_ref, o_ref):
    o_ref[...] = _rms_body(x_ref[...], w_ref[...]).astype(o_ref.dtype)


def rmsnorm(x, w, out_dtype, tm=256):
    m, d = x.shape
    return pl.pallas_call(
        _rmsnorm_kernel,
        grid=(m // tm,),
        in_specs=[pl.BlockSpec((tm, d), lambda i: (i, 0)), pl.BlockSpec((1, d), lambda i: (0, 0))],
        out_specs=pl.BlockSpec((tm, d), lambda i: (i, 0)),
        out_shape=jax.ShapeDtypeStruct((m, d), out_dtype),
        compiler_params=_cparams(("parallel",)),
        name="rmsnorm",
    )(x, w.reshape(1, d))


def _add_t_rmsnorm_kernel(x_ref, rt_ref, w_ref, xo_ref, n_ref):
    xn = x_ref[...] + rt_ref[...].T
    xo_ref[...] = xn
    n_ref[...] = _rms_body(xn, w_ref[...]).astype(n_ref.dtype)


def add_t_rmsnorm(x, r_t, w, tm=256):
    m, d = x.shape
    return pl.pallas_call(
        _add_t_rmsnorm_kernel,
        grid=(m // tm,),
        in_specs=[pl.BlockSpec((tm, d), lambda i: (i, 0)), pl.BlockSpec((d, tm), lambda i: (0, i)),
                  pl.BlockSpec((1, d), lambda i: (0, 0))],
        out_specs=[pl.BlockSpec((tm, d), lambda i: (i, 0)), pl.BlockSpec((tm, d), lambda i: (i, 0))],
        out_shape=[jax.ShapeDtypeStruct((m, d), F32), jax.ShapeDtypeStruct((m, d), BF16)],
        compiler_params=_cparams(("parallel",)),
        name="add_t_rmsnorm",
    )(x, r_t, w.reshape(1, d))


def _mm_kernel(*refs, n_dots, n_extras, epilogue):
    lhs = refs[:n_dots]
    rhs = refs[n_dots:2 * n_dots]
    extras = refs[2 * n_dots:2 * n_dots + n_extras]
    o_ref = refs[2 * n_dots + n_extras]
    dots = [_dot(a[...], b[...]) for a, b in zip(lhs, rhs)]
    o_ref[...] = epilogue(dots, [e[...] for e in extras]).astype(o_ref.dtype)


def matmul(lhs, rhs, extras=(), extra_col_offsets=None, epilogue=None, out_dtype=F32, tm=1024, tn=512,
           name="matmul"):
    m = lhs[0].shape[0]
    n = rhs[0].shape[1]
    tm = min(tm, m)
    tn = min(tn, n)
    if epilogue is None:
        epilogue = lambda dots, extras: dots[0]
    if extra_col_offsets is None:
        extra_col_offsets = (0,) * len(extras)
    in_specs = [pl.BlockSpec((tm, a.shape[1]), lambda i, j: (i, 0)) for a in lhs]
    in_specs += [pl.BlockSpec((b.shape[0], tn), lambda i, j: (0, j)) for b in rhs]
    for off in extra_col_offsets:
        in_specs.append(pl.BlockSpec((tm, tn), functools.partial(lambda i, j, o: (i, j + o), o=off // tn)))
    return pl.pallas_call(
        functools.partial(_mm_kernel, n_dots=len(lhs), n_extras=len(extras), epilogue=epilogue),
        grid=(m // tm, n // tn),
        in_specs=in_specs,
        out_specs=pl.BlockSpec((tm, tn), lambda i, j: (i, j)),
        out_shape=jax.ShapeDtypeStruct((m, n), out_dtype),
        compiler_params=_cparams(("parallel", "parallel")),
        name=name,
    )(*lhs, *rhs, *extras)


def _gate_prep_kernel(s_ref, alog_ref, dtb_ref, wa2_ref, ba_ref, beta_ref, ld_ref, la_ref):
    s = s_ref[...]
    beta_ref[...] = _sigmoid(s[:, 0:GDN_HEADS])
    dt = s[:, GDN_HEADS:2 * GDN_HEADS]
    ld_ref[...] = -jnp.exp(alog_ref[...]) * _softplus(dt + dtb_ref[...])
    logit = _bdot(s, wa2_ref[...]) + ba_ref[...]
    la_ref[...] = -_softplus(-logit) / GLA_TAU


def gate_prep(small, a_log, dt_bias, w_a2_pad, b_a, tm=512):
    m = small.shape[0]
    tm = min(tm, m)
    nqk = w_a2_pad.shape[1]
    row = lambda i: (i, 0)
    fixed = lambda i: (0, 0)
    return pl.pallas_call(
        _gate_prep_kernel,
        grid=(m // tm,),
        in_specs=[pl.BlockSpec((tm, LANES), row), pl.BlockSpec((1, GDN_HEADS), fixed),
                  pl.BlockSpec((1, GDN_HEADS), fixed), pl.BlockSpec((LANES, nqk), fixed),
                  pl.BlockSpec((1, nqk), fixed)],
        out_specs=[pl.BlockSpec((tm, GDN_HEADS), row), pl.BlockSpec((tm, GDN_HEADS), row),
                   pl.BlockSpec((tm, nqk), row)],
        out_shape=[jax.ShapeDtypeStruct((m, GDN_HEADS), F32), jax.ShapeDtypeStruct((m, GDN_HEADS), F32),
                   jax.ShapeDtypeStruct((m, nqk), F32)],
        compiler_params=_cparams(("parallel",)),
        name="gate_prep",
    )(small, a_log.reshape(1, -1), dt_bias.reshape(1, -1), w_a2_pad, b_a.reshape(1, -1))


def _conv_silu(ref, w_ref, c, C):
    start = pl.multiple_of(c * C, 8)
    cur = ref[pl.ds(start, C), :]
    pstart = pl.multiple_of(jnp.maximum(c * C - 8, 0), 8)
    prev = ref[pl.ds(pstart, 8), :] * (c > 0).astype(F32)
    z = jnp.concatenate([prev, cur], axis=0)
    w = w_ref[...]
    acc = cur * w[CONV_W - 1:CONV_W, :]
    for s in range(1, CONV_W):
        acc = acc + pltpu.roll(z, s, 0)[8:, :] * w[CONV_W - 1 - s:CONV_W - s, :]
    return _silu(acc)


def _l2norm(x):
    return x * lax.rsqrt(jnp.sum(x * x, axis=-1, keepdims=True) + EPS)


def _gdn_kernel(q_ref, k_ref, v_ref, z_ref, wq_ref, wk_ref, wv_ref, ld_ref, beta_ref, nw_ref, o_ref,
                q_s, k_s, u_s, w_s, a_s, g_s, *, C):
    n_chunks = q_ref.shape[0] // C
    D = GDN_D
    ri = lax.broadcasted_iota(jnp.int32, (C, C), 0)
    ci = lax.broadcasted_iota(jnp.int32, (C, C), 1)
    causal = ri >= ci
    strict = ri > ci
    eye = (ri == ci).astype(F32)
    ones_cc = jnp.ones((C, C), F32)
    ones_cd = jnp.ones((C, D), F32)

    def prep(c, carry):
        rows = pl.ds(pl.multiple_of(c * C, 8), C)
        q = _l2norm(_conv_silu(q_ref, wq_ref, c, C)) * (D ** -0.5)
        k = _l2norm(_conv_silu(k_ref, wk_ref, c, C))
        v = _conv_silu(v_ref, wv_ref, c, C)
        ld_m = jnp.where(causal, jnp.broadcast_to(ld_ref[pl.ds(c, 1), :], (C, C)), 0.0)
        beta_row = jnp.broadcast_to(beta_ref[pl.ds(c, 1), :], (C, C))
        g_col = _dot(ld_m, ones_cc, precision=HI)
        g_row = _dot(ones_cc, ld_m, NT, precision=HI)
        g_col_d = _dot(ld_m, ones_cd, precision=HI)
        beta_col = _dot(eye, beta_row, NT, precision=HI)
        decay = jnp.exp(jnp.where(causal, g_col - g_row, -jnp.inf))
        kb = k.astype(BF16)
        kk = _dot(kb, kb, NT)
        n_mat = -jnp.where(strict, kk * beta_col * decay, 0.0)
        t_inv = eye + n_mat
        power = n_mat
        for _ in range(int(math.log2(C)) - 1):
            pb = power.astype(BF16)
            power = _dot(pb, pb)
            t_inv = t_inv + _bdot(t_inv, power)
        u = _bdot(t_inv * beta_row, v)
        w = _bdot(t_inv * (beta_row * jnp.exp(g_row)), kb)
        attn = _dot(q.astype(BF16), kb, NT) * decay
        q_s[rows, :] = q
        k_s[rows, :] = k
        u_s[rows, :] = u
        w_s[rows, :] = w
        a_s[rows, :] = attn
        g_s[rows, :] = g_col_d
        return carry

    lax.fori_loop(0, n_chunks, prep, 0)

    def step(c, state):
        rows = pl.ds(pl.multiple_of(c * C, 8), C)
        g = g_s[rows, :]
        g_last = g_s[pl.ds(c * C + C - 1, 1), :]
        k = k_s[rows, :]
        sb = state.astype(BF16)
        v_new = u_s[rows, :] - _bdot(w_s[rows, :], sb)
        o = jnp.exp(g) * _bdot(q_s[rows, :], sb) + _bdot(a_s[rows, :], v_new)
        state = state * jnp.exp(g_last) + _bdot(k, v_new * jnp.exp(g_last - g), TN)
        o = _rms_body(o, nw_ref[...]) * _silu(z_ref[rows, :])
        o_ref[rows, :] = o.astype(o_ref.dtype)
        return state

    lax.fori_loop(0, n_chunks, step, jnp.zeros((D, D), F32))


def gdn(proj, conv_w, ld, beta, norm_w, batch, seq, C=CHUNK_GDN):
    H, D = GDN_HEADS, GDN_D
    col = lambda off: pl.BlockSpec((None, seq, D), functools.partial(lambda b, h, o: (b, 0, h + o), o=off))
    cw = lambda off: pl.BlockSpec((CONV_W, D), functools.partial(lambda b, h, o: (0, h + o), o=off))
    sc = pl.BlockSpec((None, None, seq // C, C), lambda b, h: (b, h, 0, 0))
    return pl.pallas_call(
        functools.partial(_gdn_kernel, C=C),
        grid=(batch, H),
        in_specs=[col(0), col(H), col(2 * H), col(3 * H), cw(0), cw(H), cw(2 * H), sc, sc,
                  pl.BlockSpec((1, D), lambda b, h: (0, 0))],
        out_specs=pl.BlockSpec((None, seq, D), lambda b, h: (b, 0, h)),
        out_shape=jax.ShapeDtypeStruct((batch, seq, H * D), BF16),
        scratch_shapes=[pltpu.VMEM((seq, D), F32), pltpu.VMEM((seq, D), F32), pltpu.VMEM((seq, D), F32),
                        pltpu.VMEM((seq, D), F32), pltpu.VMEM((seq, C), F32), pltpu.VMEM((seq, D), F32)],
        compiler_params=_cparams(("parallel", "parallel")),
        name="gdn",
    )(proj, proj, proj, proj, conv_w, conv_w, conv_w, ld, beta, norm_w.reshape(1, D))


def _split_bf16(x):
    hi = x.astype(BF16)
    lo = (x - hi.astype(F32)).astype(BF16)
    return hi, lo


def _gla_kernel(q_ref, k_ref, v_ref, r_ref, la_ref, nw_ref, o_ref, *, C, SUB):
    n_chunks = q_ref.shape[0] // C
    L = LANES
    DV = GLA_DV
    ri = lax.broadcasted_iota(jnp.int32, (C, C), 0)
    ci = lax.broadcasted_iota(jnp.int32, (C, C), 1)
    tri = (ri >= ci).astype(F32)
    lane = lax.broadcasted_iota(jnp.int32, (1, L), 1)
    head_of_lane = lane // GLA_DK
    hmask = [(head_of_lane == h).astype(F32) for h in range(2)]
    seg = (lax.broadcasted_iota(jnp.int32, (L, L), 0) // GLA_DK
           == lax.broadcasted_iota(jnp.int32, (L, L), 1) // GLA_DK).astype(BF16)
    sub_row = lax.broadcasted_iota(jnp.int32, (SUB, L), 0)
    lane_s = lax.broadcasted_iota(jnp.int32, (SUB, L), 1)
    row_c = lax.broadcasted_iota(jnp.int32, (C, L), 0)
    stack_mask = (lax.broadcasted_iota(jnp.int32, (2 * C, L), 0) // C
                  == lax.broadcasted_iota(jnp.int32, (2 * C, L), 1) // GLA_DK).astype(F32)
    n_sub = C // SUB

    def step(c, state_t):
        rows = pl.ds(pl.multiple_of(c * C, 8), C)
        q = q_ref[rows, :] * (GLA_DK ** -0.5)
        k = k_ref[rows, :]
        b = _dot(tri, la_ref[rows, :], precision=HI)
        b_last = b[C - 1:C, :]
        qe = q * jnp.exp(b)
        blocks = []
        for I in range(n_sub):
            r0 = I * SUB
            q_i = q[r0:r0 + SUB, :]
            b_i = b[r0:r0 + SUB, :]
            a_i = jnp.zeros((SUB, L), F32)
            if I > 0:
                b_ref0 = b[r0:r0 + 1, :]
                qt = q_i * jnp.exp(b_i - b_ref0)
                kt = k * jnp.exp(jnp.where(row_c < r0, b_ref0 - b, -jnp.inf))
                kt2 = jnp.concatenate([kt, kt], axis=0) * stack_mask
                a_i = _bdot(qt, kt2, NT)
            for jj in range(SUB):
                j = r0 + jj
                y = q_i * k[j:j + 1, :] * jnp.exp(jnp.where(sub_row >= jj, b_i - b[j:j + 1, :], -jnp.inf))
                y_hi, y_lo = _split_bf16(y)
                rsum = _dot(y_hi, seg) + _dot(y_lo, seg)
                a_i = a_i + jnp.where(lane_s % C == j, rsum, 0.0)
            blocks.append(a_i)
        a_pair = jnp.concatenate(blocks, axis=0)
        kd = k * jnp.exp(b_last - b)
        outs = []
        new_state = []
        for h in range(2):
            v_h = v_ref[rows, h * DV:(h + 1) * DV]
            st = state_t[h]
            o_h = _bdot(qe * hmask[h], st, NT)
            o_h = o_h + _bdot(a_pair * hmask[h], jnp.concatenate([v_h, v_h], axis=0))
            new_state.append(st * jnp.exp(b_last) + _bdot(v_h, kd * hmask[h], TN))
            o_h = _rms_body(o_h, nw_ref[...]) * _silu(r_ref[rows, h * DV:(h + 1) * DV])
            outs.append(o_h)
        o_ref[rows, :] = jnp.concatenate(outs, axis=1).astype(o_ref.dtype)
        return jnp.stack(new_state)

    lax.fori_loop(0, n_chunks, step, jnp.zeros((2, DV, L), F32))


def gla(proj, la, norm_w, batch, seq, q_off, k_off, v_off, r_off, C=CHUNK_GLA, SUB=SUB_GLA):
    assert 2 * C == LANES and 2 * GLA_DK == LANES
    P = GLA_HEADS // 2
    qk = lambda off: pl.BlockSpec((None, seq, LANES),
                                  functools.partial(lambda b, p, o: (b, 0, p + o), o=off // LANES))
    vr = lambda off: pl.BlockSpec((None, seq, 2 * GLA_DV),
                                  functools.partial(lambda b, p, o: (b, 0, p + o), o=off // (2 * GLA_DV)))
    return pl.pallas_call(
        functools.partial(_gla_kernel, C=C, SUB=SUB),
        grid=(batch, P),
        in_specs=[qk(q_off), qk(k_off), vr(v_off), vr(r_off),
                  pl.BlockSpec((None, seq, LANES), lambda b, p: (b, 0, p)),
                  pl.BlockSpec((1, GLA_DV), lambda b, p: (0, 0))],
        out_specs=pl.BlockSpec((None, seq, 2 * GLA_DV), lambda b, p: (b, 0, p)),
        out_shape=jax.ShapeDtypeStruct((batch, seq, GLA_HEADS * GLA_DV), BF16),
        compiler_params=_cparams(("parallel", "parallel")),
        name="gla",
    )(proj, proj, proj, proj, la, norm_w.reshape(1, GLA_DV))


def _top_values(s, n):
    vals = []
    cur = s
    for r in range(n):
        m = jnp.max(cur, axis=0, keepdims=True)
        vals.append(m)
        if r + 1 < n:
            cur = jnp.where(cur == m, -jnp.inf, cur)
    return vals


def _peer_select_kernel(q_ref, k1_ref, k2_ref, a_ref, b_ref, e1_ref, e2_ref, tau_ref):
    for h in range(PEER_HEADS):
        q1 = q_ref[:, (2 * h) * LANES:(2 * h + 1) * LANES]
        q2 = q_ref[:, (2 * h + 1) * LANES:(2 * h + 2) * LANES]
        s1 = _dot(k1_ref[...], q1, NT, precision=HI)
        s2 = _dot(k2_ref[...], q2, NT, precision=HI)
        v1 = _top_values(s1, TOPK)
        v2 = _top_values(s2, TOPK)
        v2_all = jnp.concatenate(v2, axis=0)
        cand = jnp.concatenate([v1[a] + v2_all for a in range(TOPK)], axis=0)
        top = _top_values(cand, TOPK)
        z = top[0] * 0.0
        for t in top:
            z = z + jnp.exp(t - top[0])
        a_ref[h] = s1
        b_ref[h] = s2
        e1_ref[h] = jnp.exp(s1 - v1[0]) / z
        e2_ref[h] = jnp.exp(s2 - v2[0])
        tau_ref[pl.ds(h, 1), :] = top[TOPK - 1]


def peer_select(q, keys1, keys2, tm=256):
    m = q.shape[0]
    big = pl.BlockSpec((PEER_HEADS, N_KEYS, tm), lambda i: (0, 0, i))
    big_shape = jax.ShapeDtypeStruct((PEER_HEADS, N_KEYS, m), F32)
    return pl.pallas_call(
        _peer_select_kernel,
        grid=(m // tm,),
        in_specs=[pl.BlockSpec((tm, q.shape[1]), lambda i: (i, 0)),
                  pl.BlockSpec(keys1.shape, lambda i: (0, 0)), pl.BlockSpec(keys2.shape, lambda i: (0, 0))],
        out_specs=[big, big, big, big, pl.BlockSpec((PEER_HEADS, tm), lambda i: (0, i))],
        out_shape=[big_shape, big_shape, big_shape, big_shape, jax.ShapeDtypeStruct((PEER_HEADS, m), F32)],
        compiler_params=_cparams(("parallel",)),
        name="peer_select",
    )(q, keys1, keys2)


def _peer_main_kernel(hn_ref, u_ref, vt_ref, a_ref, b_ref, e1_ref, e2_ref, tau_ref, o_ref, wt_s, *, G1):
    j = pl.program_id(1)
    tm = hn_ref.shape[0]

    @pl.when(j == 0)
    def _():
        o_ref[...] = jnp.zeros_like(o_ref)

    s_t = _dot(u_ref[...], hn_ref[...], NT)
    for g in range(G1):
        for lt in range(tm // LANES):
            ls = slice(lt * LANES, (lt + 1) * LANES)
            gate = jnp.zeros((N_KEYS, LANES), F32)
            for h in range(PEER_HEADS):
                a_row = a_ref[h, g:g + 1, ls]
                e1_row = e1_ref[h, g:g + 1, ls]
                mask = (a_row + b_ref[h, :, ls]) >= tau_ref[pl.ds(h, 1), ls]
                gate = gate + jnp.where(mask, e2_ref[h, :, ls], 0.0) * e1_row
            act = _gelu_tanh(s_t[g * N_KEYS:(g + 1) * N_KEYS, ls])
            wt_s[g * N_KEYS:(g + 1) * N_KEYS, ls] = (gate * act).astype(BF16)
    o_ref[...] += _dot(vt_ref[...], wt_s[...])


def peer_main(hn, u_b, vt_b, a, b, e1, e2, tau, tm=256, G1=4):
    m, d = hn.shape
    n_exp = u_b.shape[0]
    te = G1 * N_KEYS
    tok = pl.BlockSpec((PEER_HEADS, N_KEYS, tm), lambda i, j: (0, 0, i))
    grp = pl.BlockSpec((PEER_HEADS, None, G1, tm), lambda i, j: (0, j, 0, i))
    a = a.reshape(PEER_HEADS, N_KEYS // G1, G1, m)
    e1 = e1.reshape(PEER_HEADS, N_KEYS // G1, G1, m)
    return pl.pallas_call(
        functools.partial(_peer_main_kernel, G1=G1),
        grid=(m // tm, n_exp // te),
        in_specs=[pl.BlockSpec((tm, d), lambda i, j: (i, 0)), pl.BlockSpec((te, d), lambda i, j: (j, 0)),
                  pl.BlockSpec((d, te), lambda i, j: (0, j)), grp, tok, grp, tok,
                  pl.BlockSpec((PEER_HEADS, tm), lambda i, j: (0, i))],
        out_specs=pl.BlockSpec((d, tm), lambda i, j: (0, i)),
        out_shape=jax.ShapeDtypeStruct((d, m), F32),
        scratch_shapes=[pltpu.VMEM((te, tm), BF16)],
        compiler_params=_cparams(("parallel", "arbitrary")),
        name="peer_main",
    )(hn, u_b, vt_b, a, b, e1, e2, tau)


def _merge_epilogue(dots, extras):
    return _sigmoid(extras[0]) * dots[0] + _sigmoid(extras[1]) * dots[1]


def _residual_epilogue(dots, extras):
    return extras[0] + dots[0]


def _ple_epilogue(dots, extras):
    return extras[0] + _sigmoid(dots[0]) * dots[1]


def _layer(x, p_i, batch, seq, norm_mix, w_in, gdn_conv, gdn_a_log, gdn_dt_bias, gdn_norm, gla_w_a2, gla_b_a,
           gla_norm, w_o_gdn, w_o_gla, w_out, norm_ffn, peer_w_query, peer_keys1, peer_keys2, peer_u, peer_v,
           norm_ple, w_ple, w_ple_gate):
    m, d = x.shape
    gdn_w = 4 * GDN_HEADS * GDN_D
    gla_w = 2 * GLA_HEADS * GLA_DK + 2 * GLA_HEADS * GLA_DV
    small0 = gdn_w
    gla0 = small0 + 2 * GDN_HEADS
    a10 = gla0 + gla_w
    gate0 = a10 + GLA_RANK

    w_main = jnp.concatenate([w_in[:, :gdn_w], w_in[:, gla0:a10], w_in[:, gate0:]], axis=1).astype(BF16)
    w_small = jnp.concatenate(
        [w_in[:, small0:gla0], w_in[:, a10:gate0],
         jnp.zeros((d, LANES - 2 * GDN_HEADS - GLA_RANK), w_in.dtype)], axis=1).astype(BF16)
    w_a2_pad = jnp.zeros((LANES, gla_w_a2.shape[1]), F32).at[2 * GDN_HEADS:2 * GDN_HEADS + GLA_RANK].set(gla_w_a2)

    h = rmsnorm(x, norm_mix, BF16)
    proj = matmul([h], [w_main], name="in_proj")
    small = matmul([h], [w_small], name="in_proj_small")
    beta, ld, la = gate_prep(small, gdn_a_log, gdn_dt_bias, w_a2_pad, gla_b_a)

    proj3 = proj.reshape(batch, seq, proj.shape[1])
    to_rows = lambda t: t.reshape(batch, seq, GDN_HEADS).transpose(0, 2, 1).reshape(
        batch, GDN_HEADS, seq // CHUNK_GDN, CHUNK_GDN)
    o_gdn = gdn(proj3, gdn_conv, to_rows(ld), to_rows(beta), gdn_norm, batch, seq)
    q_off = gdn_w
    k_off = q_off + GLA_HEADS * GLA_DK
    v_off = k_off + GLA_HEADS * GLA_DK
    r_off = v_off + GLA_HEADS * GLA_DV
    o_gla = gla(proj3, la.reshape(batch, seq, -1), gla_norm, batch, seq, q_off, k_off, v_off, r_off)

    g_off = gdn_w + gla_w
    merged = matmul([o_gdn.reshape(m, -1), o_gla.reshape(m, -1)], [w_o_gdn.astype(BF16), w_o_gla.astype(BF16)],
                    extras=[proj, proj], extra_col_offsets=[g_off, g_off + d], epilogue=_merge_epilogue,
                    out_dtype=BF16, name="merge")
    x = matmul([merged], [w_out.astype(BF16)], extras=[x], epilogue=_residual_epilogue, name="out_proj")

    hn = rmsnorm(x, norm_ffn, BF16)
    qp = matmul([hn], [peer_w_query.astype(BF16)], name="peer_query")
    a, b, e1, e2, tau = peer_select(qp, peer_keys1, peer_keys2)
    peer_t = peer_main(hn, peer_u.astype(BF16), peer_v.T.astype(BF16), a, b, e1, e2, tau)
    x, n = add_t_rmsnorm(x, peer_t, norm_ple)

    x = matmul([n, p_i.astype(BF16)], [w_ple_gate.astype(BF16), w_ple.astype(BF16)], extras=[x],
               epilogue=_ple_epilogue, name="ple")
    return x


def kernel(x, p, norm_mix, w_in, gdn_conv, gdn_a_log, gdn_dt_bias, gdn_norm, gla_w_a2, gla_b_a, gla_norm,
           w_o_gdn, w_o_gla, w_out, norm_ffn, peer_w_query, peer_keys1, peer_keys2, peer_u, peer_v, norm_ple,
           w_ple, w_ple_gate, norm_final):
    batch, seq, d = x.shape
    depth = p.shape[0]
    xf = x.reshape(batch * seq, d)
    for i in range(depth):
        xf = _layer(xf, p[i].reshape(batch * seq, -1), batch, seq, norm_mix[i], w_in[i], gdn_conv[i],
                    gdn_a_log[i], gdn_dt_bias[i], gdn_norm[i], gla_w_a2[i], gla_b_a[i], gla_norm[i],
                    w_o_gdn[i], w_o_gla[i], w_out[i], norm_ffn[i], peer_w_query[i], peer_keys1[i],
                    peer_keys2[i], peer_u[i], peer_v[i], norm_ple[i], w_ple[i], w_ple_gate[i])
    return rmsnorm(xf, norm_final, F32).reshape(batch, seq, d)
```

```python
import functools
import math

import jax
import jax.numpy as jnp
from jax import lax
from jax.experimental import pallas as pl
from jax.experimental.pallas import tpu as pltpu

EPS = 1e-6
CHUNK_GDN = 64
CHUNK_GLA = 64
SUB_GLA = 16
CONV_W = 4
GDN_HEADS = 16
GDN_D = 128
GLA_HEADS = 16
GLA_DK = 64
GLA_DV = 128
GLA_RANK = 16
GLA_TAU = 16.0
PEER_HEADS = 8
N_KEYS = 128
TOPK = 16
PEER_SPLIT = 4
LANES = 128
VMEM_LIMIT = 56 * 1024 * 1024

F32 = jnp.float32
BF16 = jnp.bfloat16
HI = lax.Precision.HIGHEST
NT = (((1,), (1,)), ((), ()))
TN = (((0,), (0,)), ((), ()))


def _cparams(sem):
    return pltpu.CompilerParams(dimension_semantics=sem, vmem_limit_bytes=VMEM_LIMIT)


def _dot(a, b, dims=None, precision=None):
    if dims is None:
        return jnp.dot(a, b, preferred_element_type=F32, precision=precision)
    return lax.dot_general(a, b, dims, preferred_element_type=F32, precision=precision)


def _bdot(a, b, dims=None):
    return _dot(a.astype(BF16), b.astype(BF16), dims)


def _sigmoid(x):
    return 1.0 / (1.0 + jnp.exp(-x))


def _silu(x):
    return x * _sigmoid(x)


def _softplus(x):
    return jnp.maximum(x, 0.0) + jnp.log(1.0 + jnp.exp(-jnp.abs(x)))


def _gelu_tanh(x):
    return 0.5 * x * (1.0 + jnp.tanh(math.sqrt(2.0 / math.pi) * (x + 0.044715 * (x * x * x))))


def _split3(x):
    hi = x.astype(BF16)
    r = x - hi.astype(F32)
    mid = r.astype(BF16)
    lo = (r - mid.astype(F32)).astype(BF16)
    return hi, mid, lo


def _rms_body(x, w):
    return x * lax.rsqrt(jnp.mean(x * x, axis=-1, keepdims=True) + EPS) * w


def _rmsnorm_kernel(x_ref, w_ref, o_ref):
    o_ref[...] = _rms_body(x_ref[...], w_ref[...]).astype(o_ref.dtype)


def rmsnorm(x, w, out_dtype, tm=256):
    m, d = x.shape
    return pl.pallas_call(
        _rmsnorm_kernel,
        grid=(m // tm,),
        in_specs=[pl.BlockSpec((tm, d), lambda i: (i, 0)), pl.BlockSpec((1, d), lambda i: (0, 0))],
        out_specs=pl.BlockSpec((tm, d), lambda i: (i, 0)),
        out_shape=jax.ShapeDtypeStruct((m, d), out_dtype),
        compiler_params=_cparams(("parallel",)),
        name="rmsnorm",
    )(x, w.reshape(1, d))


def _add_t_rmsnorm_kernel(x_ref, rt_ref, w_ref, xo_ref, n_ref):
    xn = x_ref[...] + rt_ref[...].T
    xo_ref[...] = xn
    n_ref[...] = _rms_body(xn, w_ref[...]).astype(n_ref.dtype)


def add_t_rmsnorm(x, r_t, w, tm=256):
    m, d = x.shape
    return pl.pallas_call(
        _add_t_rmsnorm_kernel,
        grid=(m // tm,),
        in_specs=[pl.BlockSpec((tm, d), lambda i: (i, 0)), pl.BlockSpec((d, tm), lambda i: (0, i)),
                  pl.BlockSpec((1, d), lambda i: (0, 0))],
        out_specs=[pl.BlockSpec((tm, d), lambda i: (i, 0)), pl.BlockSpec((tm, d), lambda i: (i, 0))],
        out_shape=[jax.ShapeDtypeStruct((m, d), F32), jax.ShapeDtypeStruct((m, d), BF16)],
        compiler_params=_cparams(("parallel",)),
        name="add_t_rmsnorm",
    )(x, r_t, w.reshape(1, d))


def _mm_kernel(*refs, n_dots, n_extras, epilogue):
    lhs = refs[:n_dots]
    rhs = refs[n_dots:2 * n_dots]
    extras = refs[2 * n_dots:2 * n_dots + n_extras]
    o_ref = refs[2 * n_dots + n_extras]
    dots = [_dot(a[...], b[...]) for a, b in zip(lhs, rhs)]
    o_ref[...] = epilogue(dots, [e[...] for e in extras]).astype(o_ref.dtype)


def matmul(lhs, rhs, extras=(), extra_col_offsets=None, epilogue=None, out_dtype=F32, tm=1024, tn=512,
           name="matmul"):
    m = lhs[0].shape[0]
    n = rhs[0].shape[1]
    tm = min(tm, m)
    tn = min(tn, n)
    if epilogue is None:
        epilogue = lambda dots, extras: dots[0]
    if extra_col_offsets is None:
        extra_col_offsets = (0,) * len(extras)
    in_specs = [pl.BlockSpec((tm, a.shape[1]), lambda i, j: (i, 0)) for a in lhs]
    in_specs += [pl.BlockSpec((b.shape[0], tn), lambda i, j: (0, j)) for b in rhs]
    for off in extra_col_offsets:
        in_specs.append(pl.BlockSpec((tm, tn), functools.partial(lambda i, j, o: (i, j + o), o=off // tn)))
    return pl.pallas_call(
        functools.partial(_mm_kernel, n_dots=len(lhs), n_extras=len(extras), epilogue=epilogue),
        grid=(m // tm, n // tn),
        in_specs=in_specs,
        out_specs=pl.BlockSpec((tm, tn), lambda i, j: (i, j)),
        out_shape=jax.ShapeDtypeStruct((m, n), out_dtype),
        compiler_params=_cparams(("parallel", "parallel")),
        name=name,
    )(*lhs, *rhs, *extras)


TOK_BETA = 0
TOK_LD = GDN_HEADS
TOK_G = 2 * GDN_HEADS


def _gate_prep_kernel(s_ref, alog_ref, dtb_ref, wa2_ref, ba_ref, tok_ref, la_ref, *, C):
    tm = s_ref.shape[0]
    s = s_ref[...]
    lane = lax.broadcasted_iota(jnp.int32, s.shape, 1)
    ld = jnp.where((lane >= TOK_LD) & (lane < TOK_G), -jnp.exp(alog_ref[...]) * _softplus(s + dtb_ref[...]), 0.0)
    ri = lax.broadcasted_iota(jnp.int32, (tm, tm), 0)
    ci = lax.broadcasted_iota(jnp.int32, (tm, tm), 1)
    tri = ((ri // C == ci // C) & (ci <= ri)).astype(F32)
    g = _dot(tri, ld, precision=HI)
    tok_ref[...] = jnp.where(lane < TOK_LD, _sigmoid(s), ld) + pltpu.roll(g, TOK_G - TOK_LD, 1)
    logit = _bdot(s, wa2_ref[...]) + ba_ref[...]
    la_ref[...] = -_softplus(-logit) / GLA_TAU


def gate_prep(small, a_log, dt_bias, w_a2_pad, b_a, tm=512):
    m = small.shape[0]
    tm = min(tm, m)
    nqk = w_a2_pad.shape[1]
    row = lambda i: (i, 0)
    fixed = lambda i: (0, 0)
    pad = lambda v: jnp.zeros((1, LANES), F32).at[0, TOK_LD:TOK_G].set(v)
    return pl.pallas_call(
        functools.partial(_gate_prep_kernel, C=CHUNK_GDN),
        grid=(m // tm,),
        in_specs=[pl.BlockSpec((tm, LANES), row), pl.BlockSpec((1, LANES), fixed),
                  pl.BlockSpec((1, LANES), fixed), pl.BlockSpec((LANES, nqk), fixed),
                  pl.BlockSpec((1, nqk), fixed)],
        out_specs=[pl.BlockSpec((tm, LANES), row), pl.BlockSpec((tm, nqk), row)],
        out_shape=[jax.ShapeDtypeStruct((m, LANES), F32), jax.ShapeDtypeStruct((m, nqk), F32)],
        compiler_params=_cparams(("parallel",)),
        name="gate_prep",
    )(small, pad(a_log), pad(dt_bias), w_a2_pad, b_a.reshape(1, -1))


def _conv_silu(ref, w_ref, tail_ref, idx, c, C, lanes):
    start = pl.multiple_of(c * C, 8)
    cur = ref[pl.ds(start, C), lanes]
    pstart = pl.multiple_of(jnp.maximum(c * C - 8, 0), 8)
    prev = jnp.where(c > 0, ref[pl.ds(pstart, 8), lanes], tail_ref[idx, :, lanes])
    z = jnp.concatenate([prev, cur], axis=0)
    w = w_ref[:, lanes]
    acc = cur * w[CONV_W - 1:CONV_W, :]
    for s in range(1, CONV_W):
        acc = acc + pltpu.roll(z, s, 0)[8:, :] * w[CONV_W - 1 - s:CONV_W - s, :]
    return _silu(acc)


def _l2norm(x):
    return x * lax.rsqrt(jnp.sum(x * x, axis=-1, keepdims=True) + EPS)


def _gdn_kernel(q_ref, k_ref, v_ref, z_ref, wq_ref, wk_ref, wv_ref, tok_ref, brow_ref, grow_ref, nw_ref, o_ref,
                state_s, tail_s, col_s, q_s, k_s, u_s, w_s, a_s, *, C, G, HG):
    grp = pl.program_id(1)
    sb = pl.program_id(2)
    TS = q_ref.shape[0]
    n_chunks = TS // C
    D = GDN_D
    ri = lax.broadcasted_iota(jnp.int32, (C, C), 0)
    ci = lax.broadcasted_iota(jnp.int32, (C, C), 1)
    causal = ri >= ci
    strict = ri > ci
    eye = (ri == ci).astype(F32)

    @pl.when(sb == 0)
    def _():
        state_s[...] = jnp.zeros_like(state_s)
        tail_s[...] = jnp.zeros_like(tail_s)

    srow = lax.broadcasted_iota(jnp.int32, (LANES, 2 * D), 0)
    scol = lax.broadcasted_iota(jnp.int32, (LANES, 2 * D), 1)
    slab = min(TS, 256)
    for hh in range(HG):
        h = grp * HG + hh
        sel = (srow == jnp.where(scol < D, TOK_BETA + h, TOK_G + h)).astype(F32).astype(BF16)
        for r0 in range(0, TS, slab):
            hi, mid, lo = _split3(tok_ref[r0:r0 + slab, :])
            col_s[hh, r0:r0 + slab, :] = _dot(hi, sel) + _dot(mid, sel) + _dot(lo, sel)

    heads = range(HG)
    head_lanes = [slice(hh * D, (hh + 1) * D) for hh in heads]

    def prep(i, carry):
        items = [(i * G + gg, hh) for gg in range(G) for hh in heads]
        rows = [pl.ds(pl.multiple_of(c * C, 8), C) for c, _ in items]
        k = [_l2norm(_conv_silu(k_ref, wk_ref, tail_s, 1, c, C, head_lanes[hh])) for c, hh in items]
        kb = [x.astype(BF16) for x in k]
        kk = [_dot(x, x, NT) for x in kb]
        beta_row = [jnp.broadcast_to(brow_ref[hh, c], (C, C)) for c, hh in items]
        g_row = [jnp.broadcast_to(grow_ref[hh, c], (C, C)) for c, hh in items]
        decay = [jnp.exp(jnp.where(causal, col_s[hh, r, D:D + C] - gr, -jnp.inf))
                 for (c, hh), r, gr in zip(items, rows, g_row)]
        n_mat = [-jnp.where(strict, x * col_s[hh, r, 0:C] * dc, 0.0)
                 for (c, hh), r, x, dc in zip(items, rows, kk, decay)]
        t_inv = [eye + x for x in n_mat]
        power = n_mat
        v = [_conv_silu(v_ref, wv_ref, tail_s, 2, c, C, head_lanes[hh]) for c, hh in items]
        q = None
        for level in range(int(math.log2(C)) - 1):
            pb = [x.astype(BF16) for x in power]
            power = [_dot(x, x) for x in pb]
            t_inv = [t + _bdot(t, x) for t, x in zip(t_inv, power)]
            if level == 1:
                q = [_l2norm(_conv_silu(q_ref, wq_ref, tail_s, 0, c, C, head_lanes[hh])) * (D ** -0.5)
                     for c, hh in items]
        for n, (c, hh) in enumerate(items):
            r = rows[n]
            q_s[hh, r, :] = q[n]
            k_s[hh, r, :] = k[n]
            u_s[hh, r, :] = _bdot(t_inv[n] * beta_row[n], v[n])
            w_s[hh, r, :] = _bdot(t_inv[n] * (beta_row[n] * jnp.exp(g_row[n])), kb[n])
            a_s[hh, r, :] = _dot(q[n].astype(BF16), kb[n], NT) * decay[n]
        return carry

    lax.fori_loop(0, n_chunks // G, prep, 0)

    def step(c, carry):
        rows = pl.ds(pl.multiple_of(c * C, 8), C)
        g = [col_s[hh, rows, D:] for hh in heads]
        state = [state_s[hh] for hh in heads]
        sb16 = [x.astype(BF16) for x in state]
        v_new = [u_s[hh, rows, :] - _bdot(w_s[hh, rows, :], sb16[hh]) for hh in heads]
        inter = [_bdot(q_s[hh, rows, :], sb16[hh]) for hh in heads]
        for hh in heads:
            g_last = g[hh][C - 1:C, :]
            state_s[hh] = state[hh] * jnp.exp(g_last) + _bdot(k_s[hh, rows, :],
                                                             v_new[hh] * jnp.exp(g_last - g[hh]), TN)
        for hh in heads:
            o = jnp.exp(g[hh]) * inter[hh] + _bdot(a_s[hh, rows, :], v_new[hh])
            o = _rms_body(o, nw_ref[...]) * _silu(z_ref[rows, head_lanes[hh]])
            o_ref[rows, head_lanes[hh]] = o.astype(o_ref.dtype)
        return carry

    lax.fori_loop(0, n_chunks, step, 0)
    tail_s[0] = q_ref[TS - 8:TS, :]
    tail_s[1] = k_ref[TS - 8:TS, :]
    tail_s[2] = v_ref[TS - 8:TS, :]


def gdn(proj, conv_w, tok, brow, grow, norm_w, batch, seq, C=CHUNK_GDN, TS=1024, G=4, HG=4):
    H, D = GDN_HEADS, GDN_D
    TS = min(TS, seq)
    W = HG * D
    col = lambda off: pl.BlockSpec((None, TS, W), functools.partial(lambda b, g, s, o: (b, s, g + o), o=off // W))
    cw = lambda off: pl.BlockSpec((CONV_W, W), functools.partial(lambda b, g, s, o: (0, g + o), o=off // W))
    rows = pl.BlockSpec((None, HG, TS // C, 1, C), lambda b, g, s: (b, g, s, 0, 0))
    blk = lambda lanes: pltpu.VMEM((HG, TS, lanes), F32)
    return pl.pallas_call(
        functools.partial(_gdn_kernel, C=C, G=G, HG=HG),
        grid=(batch, H // HG, seq // TS),
        in_specs=[col(0), col(H * D), col(2 * H * D), col(3 * H * D), cw(0), cw(H * D), cw(2 * H * D),
                  pl.BlockSpec((None, TS, LANES), lambda b, g, s: (b, s, 0)), rows, rows,
                  pl.BlockSpec((1, D), lambda b, g, s: (0, 0))],
        out_specs=pl.BlockSpec((None, TS, W), lambda b, g, s: (b, s, g)),
        out_shape=jax.ShapeDtypeStruct((batch, seq, H * D), BF16),
        scratch_shapes=[pltpu.VMEM((HG, D, D), F32), pltpu.VMEM((3, 8, W), F32), blk(2 * D),
                        blk(D), blk(D), blk(D), blk(D), blk(C)],
        compiler_params=_cparams(("parallel", "parallel", "arbitrary")),
        name="gdn",
    )(proj, proj, proj, proj, conv_w, conv_w, conv_w, tok, brow, grow, norm_w.reshape(1, D))


def _split_bf16(x):
    hi = x.astype(BF16)
    lo = (x - hi.astype(F32)).astype(BF16)
    return hi, lo


def _gla_kernel(q_ref, k_ref, v_ref, r_ref, la_ref, nw_ref, o_ref, *, C, SUB, G):
    n_chunks = q_ref.shape[0] // C
    L = LANES
    DV = GLA_DV
    ri = lax.broadcasted_iota(jnp.int32, (C, C), 0)
    ci = lax.broadcasted_iota(jnp.int32, (C, C), 1)
    tri = (ri >= ci).astype(F32).astype(BF16)
    lane = lax.broadcasted_iota(jnp.int32, (1, L), 1)
    head_of_lane = lane // GLA_DK
    hmask = [(head_of_lane == h).astype(F32) for h in range(2)]
    seg = (lax.broadcasted_iota(jnp.int32, (L, L), 0) // GLA_DK
           == lax.broadcasted_iota(jnp.int32, (L, L), 1) // GLA_DK).astype(F32).astype(BF16)
    sub_row = lax.broadcasted_iota(jnp.int32, (SUB, L), 0)
    lane_s = lax.broadcasted_iota(jnp.int32, (SUB, L), 1)
    row_c = lax.broadcasted_iota(jnp.int32, (C, L), 0)
    stack_mask = (lax.broadcasted_iota(jnp.int32, (2 * C, L), 0) // C
                  == lax.broadcasted_iota(jnp.int32, (2 * C, L), 1) // GLA_DK).astype(F32)
    n_sub = C // SUB
    chunks = range(G)

    def step(i, state_t):
        rows = [pl.ds(pl.multiple_of((i * G + n) * C, 8), C) for n in chunks]
        q = [q_ref[r, :] * (GLA_DK ** -0.5) for r in rows]
        k = [k_ref[r, :] for r in rows]
        b = []
        for r in rows:
            hi, mid, lo = _split3(la_ref[r, :])
            b.append(_dot(tri, hi) + _dot(tri, mid) + _dot(tri, lo))
        blocks = [[jnp.zeros((SUB, L), F32)] + [None] * (n_sub - 1) for _ in chunks]
        for I in range(1, n_sub):
            r0 = I * SUB
            for n in chunks:
                b_ref0 = b[n][r0:r0 + 1, :]
                qt = q[n][r0:r0 + SUB, :] * jnp.exp(b[n][r0:r0 + SUB, :] - b_ref0)
                kt = k[n] * jnp.exp(jnp.where(row_c < r0, b_ref0 - b[n], -jnp.inf))
                kt2 = jnp.concatenate([kt, kt], axis=0) * stack_mask
                blocks[n][I] = _bdot(qt, kt2, NT)
        rsum = []
        for n in chunks:
            ys = []
            for I in range(n_sub):
                r0 = I * SUB
                q_i = q[n][r0:r0 + SUB, :]
                b_i = b[n][r0:r0 + SUB, :]
                for jj in range(SUB):
                    j = r0 + jj
                    ys.append(q_i * k[n][j:j + 1, :]
                              * jnp.exp(jnp.where(sub_row >= jj, b_i - b[n][j:j + 1, :], -jnp.inf)))
            y_hi, y_lo = _split_bf16(jnp.concatenate(ys, axis=0))
            rsum.append(_dot(y_hi, seg) + _dot(y_lo, seg))
        a_pair = []
        for n in chunks:
            for I in range(n_sub):
                acc = blocks[n][I]
                for jj in range(SUB):
                    j = I * SUB + jj
                    acc = acc + jnp.where(lane_s % C == j, rsum[n][j * SUB:(j + 1) * SUB, :], 0.0)
                blocks[n][I] = acc
            a_pair.append(jnp.concatenate(blocks[n], axis=0))
        v = [[v_ref[r, h * DV:(h + 1) * DV] for h in range(2)] for r in rows]
        intra = [[_bdot(a_pair[n] * hmask[h], jnp.concatenate([v[n][h], v[n][h]], axis=0)) for h in range(2)]
                 for n in chunks]
        kv = [[_bdot(v[n][h], k[n] * jnp.exp(b[n][C - 1:C, :] - b[n]) * hmask[h], TN) for h in range(2)]
              for n in chunks]
        state = [state_t[0], state_t[1]]
        for n in chunks:
            qe = q[n] * jnp.exp(b[n])
            outs = []
            for h in range(2):
                o_h = _bdot(qe * hmask[h], state[h], NT) + intra[n][h]
                state[h] = state[h] * jnp.exp(b[n][C - 1:C, :]) + kv[n][h]
                outs.append(_rms_body(o_h, nw_ref[...]) * _silu(r_ref[rows[n], h * DV:(h + 1) * DV]))
            o_ref[rows[n], :] = jnp.concatenate(outs, axis=1).astype(o_ref.dtype)
        return jnp.stack(state)

    lax.fori_loop(0, n_chunks // G, step, jnp.zeros((2, DV, L), F32))


def gla(proj, la, norm_w, batch, seq, q_off, k_off, v_off, r_off, C=CHUNK_GLA, SUB=SUB_GLA, G=4):
    assert 2 * C == LANES and 2 * GLA_DK == LANES
    P = GLA_HEADS // 2
    qk = lambda off: pl.BlockSpec((None, seq, LANES),
                                  functools.partial(lambda b, p, o: (b, 0, p + o), o=off // LANES))
    vr = lambda off: pl.BlockSpec((None, seq, 2 * GLA_DV),
                                  functools.partial(lambda b, p, o: (b, 0, p + o), o=off // (2 * GLA_DV)))
    return pl.pallas_call(
        functools.partial(_gla_kernel, C=C, SUB=SUB, G=G),
        grid=(batch, P),
        in_specs=[qk(q_off), qk(k_off), vr(v_off), vr(r_off),
                  pl.BlockSpec((None, seq, LANES), lambda b, p: (b, 0, p)),
                  pl.BlockSpec((1, GLA_DV), lambda b, p: (0, 0))],
        out_specs=pl.BlockSpec((None, seq, 2 * GLA_DV), lambda b, p: (b, 0, p)),
        out_shape=jax.ShapeDtypeStruct((batch, seq, GLA_HEADS * GLA_DV), BF16),
        compiler_params=_cparams(("parallel", "parallel")),
        name="gla",
    )(proj, proj, proj, proj, la, norm_w.reshape(1, GLA_DV))


def _top_values(s, n):
    vals = []
    cur = s
    for r in range(n):
        m = jnp.max(cur, axis=0, keepdims=True)
        vals.append(m)
        if r + 1 < n:
            cur = jnp.where(cur == m, -jnp.inf, cur)
    return vals


THRESH_SLACK = 4.0 * 2.0 ** -23


def _peer_select_kernel(q_ref, k1_ref, k2_ref, th_ref, b_ref, e1_ref, e2_ref):
    for h in range(PEER_HEADS):
        q1 = q_ref[:, (2 * h) * LANES:(2 * h + 1) * LANES]
        q2 = q_ref[:, (2 * h + 1) * LANES:(2 * h + 2) * LANES]
        s1 = _dot(k1_ref[...], q1, NT, precision=HI)
        s2 = _dot(k2_ref[...], q2, NT, precision=HI)
        v1 = _top_values(s1, TOPK)
        v2 = _top_values(s2, TOPK)
        v2_all = jnp.concatenate(v2, axis=0)
        cand = jnp.concatenate([v1[a] + v2_all for a in range(TOPK)], axis=0)
        top = _top_values(cand, TOPK)
        z = top[0] * 0.0
        for t in top:
            z = z + jnp.exp(t - top[0])
        tau = top[TOPK - 1]
        outs = ((th_ref, (tau - s1) - THRESH_SLACK * (jnp.abs(tau) + jnp.abs(s1))), (b_ref, s2),
                (e1_ref, jnp.exp(s1 - v1[0]) / z), (e2_ref, jnp.exp(s2 - v2[0])))
        for ref, val in outs:
            for lt in range(val.shape[1] // LANES):
                ref[lt, h] = val[:, lt * LANES:(lt + 1) * LANES]


def peer_select(q, keys1, keys2, tm=256):
    m = q.shape[0]
    big = pl.BlockSpec((tm // LANES, PEER_HEADS, N_KEYS, LANES), lambda i: (i, 0, 0, 0))
    big_shape = jax.ShapeDtypeStruct((m // LANES, PEER_HEADS, N_KEYS, LANES), F32)
    return pl.pallas_call(
        _peer_select_kernel,
        grid=(m // tm,),
        in_specs=[pl.BlockSpec((tm, q.shape[1]), lambda i: (i, 0)),
                  pl.BlockSpec(keys1.shape, lambda i: (0, 0)), pl.BlockSpec(keys2.shape, lambda i: (0, 0))],
        out_specs=[big, big, big, big],
        out_shape=[big_shape, big_shape, big_shape, big_shape],
        compiler_params=_cparams(("parallel",)),
        name="peer_select",
    )(q, keys1, keys2)


def _peer_main_kernel(hn_ref, u_ref, vt_ref, th_ref, b_ref, e1_ref, e2_ref, o_ref, w_wr, w_rd, s_wr, s_rd, *, GR):
    j = pl.program_id(1)
    n_parts, tm, _ = hn_ref.shape
    n_first = th_ref.shape[2]
    d = o_ref.shape[0]
    rs = d // n_parts
    assert tm // LANES == n_parts

    @pl.when(j == 0)
    def _():
        o_ref[...] = jnp.zeros_like(o_ref)
        w_wr[...] = jnp.zeros_like(w_wr)
        s_wr[...] = jnp.zeros_like(s_wr)

    for lt in range(n_parts):
        ls = slice(lt * LANES, (lt + 1) * LANES)
        w_rd[:, ls] = w_wr[lt]
        s_rd[lt] = s_wr[:, ls]

    def part(t, carry):
        v_rows = pl.ds(pl.multiple_of(t * rs, rs), rs)
        o_ref[v_rows, :] += _dot(vt_ref[v_rows, :], w_rd[...])
        scores = _dot(u_ref[t], hn_ref[t], NT)
        s_wr[...] = jnp.where(t == 0, scores, scores + s_wr[...])
        for r0 in range(0, N_KEYS, GR):
            keys = slice(r0, r0 + GR)
            gate = [jnp.zeros((GR, LANES), F32) for _ in range(n_first)]
            for h in range(PEER_HEADS):
                s2 = b_ref[t, h, keys, :]
                e2 = e2_ref[t, h, keys, :]
                for g in range(n_first):
                    gate[g] = gate[g] + jnp.where(s2 >= th_ref[t, h, g:g + 1, :], e2, 0.0) * e1_ref[t, h, g:g + 1, :]
            for g in range(n_first):
                rows = slice(g * N_KEYS + r0, g * N_KEYS + r0 + GR)
                w_wr[t, rows, :] = (gate[g] * _gelu_tanh(s_rd[t, rows, :])).astype(BF16)
        return carry

    lax.fori_loop(0, n_parts, part, 0)


def peer_main(hn, u_b, vt_b, th, b, e1, e2, tm=512, G1=4):
    _, m, dk = hn.shape
    d = vt_b.shape[0]
    tm = min(tm, m)
    n_lt = tm // LANES
    n_exp = u_b.shape[1]
    te = G1 * N_KEYS
    nj = n_exp // te
    once = pl.Buffered(1)
    tok = pl.BlockSpec((n_lt, PEER_HEADS, N_KEYS, LANES), lambda i, j: (i, 0, 0, 0), pipeline_mode=once)
    chunk = lambda j, lag: jnp.clip(j - lag, 0, nj - 1)
    grp = pl.BlockSpec((n_lt, PEER_HEADS, None, G1, LANES), lambda i, j: (i, 0, chunk(j, 1), 0, 0))
    th = th.reshape(m // LANES, PEER_HEADS, N_KEYS // G1, G1, LANES)
    e1 = e1.reshape(m // LANES, PEER_HEADS, N_KEYS // G1, G1, LANES)
    return pl.pallas_call(
        functools.partial(_peer_main_kernel, GR=32),
        grid=(m // tm, nj + 2),
        in_specs=[pl.BlockSpec((G1, tm, dk), lambda i, j: (0, i, 0), pipeline_mode=once),
                  pl.BlockSpec((G1, te, dk), lambda i, j: (0, chunk(j, 0), 0)),
                  pl.BlockSpec((d, te), lambda i, j: (0, chunk(j, 2))), grp, tok, grp, tok],
        out_specs=pl.BlockSpec((d, tm), lambda i, j: (0, i)),
        out_shape=jax.ShapeDtypeStruct((d, m), F32),
        scratch_shapes=[pltpu.VMEM((n_lt, te, LANES), BF16), pltpu.VMEM((te, tm), BF16),
                        pltpu.VMEM((te, tm), F32), pltpu.VMEM((n_lt, te, LANES), F32)],
        compiler_params=_cparams(("parallel", "arbitrary")),
        name="peer_main",
    )(hn, u_b, vt_b, th, b, e1, e2)


def _merge_epilogue(dots, extras):
    return _sigmoid(extras[0]) * dots[0] + _sigmoid(extras[1]) * dots[1]


def _residual_epilogue(dots, extras):
    return extras[0] + dots[0]


def _ple_epilogue(dots, extras):
    return extras[0] + _sigmoid(dots[0]) * dots[1]


def _layer(x, p_i, batch, seq, norm_mix, w_in, gdn_conv, gdn_a_log, gdn_dt_bias, gdn_norm, gla_w_a2, gla_b_a,
           gla_norm, w_o_gdn, w_o_gla, w_out, norm_ffn, peer_w_query, peer_keys1, peer_keys2, peer_u, peer_v,
           norm_ple, w_ple, w_ple_gate):
    m, d = x.shape
    gdn_w = 4 * GDN_HEADS * GDN_D
    gla_w = 2 * GLA_HEADS * GLA_DK + 2 * GLA_HEADS * GLA_DV
    small0 = gdn_w
    gla0 = small0 + 2 * GDN_HEADS
    a10 = gla0 + gla_w
    gate0 = a10 + GLA_RANK

    w_main = jnp.concatenate([w_in[:, :gdn_w], w_in[:, gla0:a10], w_in[:, gate0:]], axis=1).astype(BF16)
    w_small = jnp.concatenate(
        [w_in[:, small0:gla0], w_in[:, a10:gate0],
         jnp.zeros((d, LANES - 2 * GDN_HEADS - GLA_RANK), w_in.dtype)], axis=1).astype(BF16)
    w_a2_pad = jnp.zeros((LANES, gla_w_a2.shape[1]), F32).at[2 * GDN_HEADS:2 * GDN_HEADS + GLA_RANK].set(gla_w_a2)

    h = rmsnorm(x, norm_mix, BF16)
    proj = matmul([h], [w_main], name="in_proj")
    small = matmul([h], [w_small], name="in_proj_small")
    tok, la = gate_prep(small, gdn_a_log, gdn_dt_bias, w_a2_pad, gla_b_a)

    proj3 = proj.reshape(batch, seq, proj.shape[1])
    to_rows = lambda t: t.reshape(batch, seq, GDN_HEADS).transpose(0, 2, 1).reshape(
        batch, GDN_HEADS, seq // CHUNK_GDN, 1, CHUNK_GDN)
    o_gdn = gdn(proj3, gdn_conv, tok.reshape(batch, seq, LANES), to_rows(tok[:, TOK_BETA:TOK_BETA + GDN_HEADS]),
                to_rows(tok[:, TOK_G:TOK_G + GDN_HEADS]), gdn_norm, batch, seq)
    q_off = gdn_w
    k_off = q_off + GLA_HEADS * GLA_DK
    v_off = k_off + GLA_HEADS * GLA_DK
    r_off = v_off + GLA_HEADS * GLA_DV
    o_gla = gla(proj3, la.reshape(batch, seq, -1), gla_norm, batch, seq, q_off, k_off, v_off, r_off)

    g_off = gdn_w + gla_w
    merged = matmul([o_gdn.reshape(m, -1), o_gla.reshape(m, -1)], [w_o_gdn.astype(BF16), w_o_gla.astype(BF16)],
                    extras=[proj, proj], extra_col_offsets=[g_off, g_off + d], epilogue=_merge_epilogue,
                    out_dtype=BF16, name="merge")
    x = matmul([merged], [w_out.astype(BF16)], extras=[x], epilogue=_residual_epilogue, name="out_proj")

    hn = rmsnorm(x, norm_ffn, BF16)
    qp = matmul([hn], [peer_w_query.astype(BF16)], name="peer_query")
    th, b, e1, e2 = peer_select(qp, peer_keys1, peer_keys2)
    split = lambda t: t.reshape(t.shape[0], PEER_SPLIT, d // PEER_SPLIT).transpose(1, 0, 2)
    peer_t = peer_main(split(hn), split(peer_u.astype(BF16)), peer_v.T.astype(BF16), th, b, e1, e2, G1=PEER_SPLIT)
    x, n = add_t_rmsnorm(x, peer_t, norm_ple)

    x = matmul([n, p_i.astype(BF16)], [w_ple_gate.astype(BF16), w_ple.astype(BF16)], extras=[x],
               epilogue=_ple_epilogue, name="ple")
    return x


def kernel(x, p, norm_mix, w_in, gdn_conv, gdn_a_log, gdn_dt_bias, gdn_norm, gla_w_a2, gla_b_a, gla_norm,
           w_o_gdn, w_o_gla, w_out, norm_ffn, peer_w_query, peer_keys1, peer_keys2, peer_u, peer_v, norm_ple,
           w_ple, w_ple_gate, norm_final):
    batch, seq, d = x.shape
    depth = p.shape[0]
    xf = x.reshape(batch * seq, d)
    for i in range(depth):
        xf = _layer(xf, p[i].reshape(batch * seq, -1), batch, seq, norm_mix[i], w_in[i], gdn_conv[i],
                    gdn_a_log[i], gdn_dt_bias[i], gdn_norm[i], gla_w_a2[i], gla_b_a[i], gla_norm[i],
                    w_o_gdn[i], w_o_gla[i], w_out[i], norm_ffn[i], peer_w_query[i], peer_keys1[i],
                    peer_keys2[i], peer_u[i], peer_v[i], norm_ple[i], w_ple[i], w_ple_gate[i])
    return rmsnorm(xf, norm_final, F32).reshape(batch, seq, d)
```

```python
import functools
import math

import jax
import jax.numpy as jnp
from jax import lax
from jax.experimental import pallas as pl
from jax.experimental.pallas import tpu as pltpu

EPS = 1e-6
CHUNK_GDN = 64
CHUNK_GLA = 64
SUB_GLA = 16
CONV_W = 4
GDN_HEADS = 16
GDN_D = 128
GLA_HEADS = 16
GLA_DK = 64
GLA_DV = 128
GLA_RANK = 16
GLA_TAU = 16.0
PEER_HEADS = 8
N_KEYS = 128
TOPK = 16
PEER_SPLIT = 4
LANES = 128
VMEM_LIMIT = 56 * 1024 * 1024

F32 = jnp.float32
BF16 = jnp.bfloat16
HI = lax.Precision.HIGHEST
NT = (((1,), (1,)), ((), ()))
TN = (((0,), (0,)), ((), ()))


def _cparams(sem):
    return pltpu.CompilerParams(dimension_semantics=sem, vmem_limit_bytes=VMEM_LIMIT)


def _dot(a, b, dims=None, precision=None):
    if dims is None:
        return jnp.dot(a, b, preferred_element_type=F32, precision=precision)
    return lax.dot_general(a, b, dims, preferred_element_type=F32, precision=precision)


def _bdot(a, b, dims=None):
    return _dot(a.astype(BF16), b.astype(BF16), dims)


def _sigmoid(x):
    return 1.0 / (1.0 + jnp.exp(-x))


def _silu(x):
    return x * _sigmoid(x)


def _softplus(x):
    return jnp.maximum(x, 0.0) + jnp.log(1.0 + jnp.exp(-jnp.abs(x)))


def _gelu_tanh(x):
    return 0.5 * x * (1.0 + jnp.tanh(math.sqrt(2.0 / math.pi) * (x + 0.044715 * (x * x * x))))


def _split3(x):
    hi = x.astype(BF16)
    r = x - hi.astype(F32)
    mid = r.astype(BF16)
    lo = (r - mid.astype(F32)).astype(BF16)
    return hi, mid, lo


def _rms_body(x, w):
    return x * lax.rsqrt(jnp.mean(x * x, axis=-1, keepdims=True) + EPS) * w


def _rmsnorm_kernel(x_ref, w_ref, o_ref):
    o_ref[...] = _rms_body(x_ref[...], w_ref[...]).astype(o_ref.dtype)


def rmsnorm(x, w, out_dtype, tm=256):
    m, d = x.shape
    return pl.pallas_call(
        _rmsnorm_kernel,
        grid=(m // tm,),
        in_specs=[pl.BlockSpec((tm, d), lambda i: (i, 0)), pl.BlockSpec((1, d), lambda i: (0, 0))],
        out_specs=pl.BlockSpec((tm, d), lambda i: (i, 0)),
        out_shape=jax.ShapeDtypeStruct((m, d), out_dtype),
        compiler_params=_cparams(("parallel",)),
        name="rmsnorm",
    )(x, w.reshape(1, d))


def _add_t_rmsnorm_kernel(x_ref, rt_ref, w_ref, xo_ref, n_ref):
    xn = x_ref[...] + rt_ref[...].T
    xo_ref[...] = xn
    n_ref[...] = _rms_body(xn, w_ref[...]).astype(n_ref.dtype)


def add_t_rmsnorm(x, r_t, w, tm=256):
    m, d = x.shape
    return pl.pallas_call(
        _add_t_rmsnorm_kernel,
        grid=(m // tm,),
        in_specs=[pl.BlockSpec((tm, d), lambda i: (i, 0)), pl.BlockSpec((d, tm), lambda i: (0, i)),
                  pl.BlockSpec((1, d), lambda i: (0, 0))],
        out_specs=[pl.BlockSpec((tm, d), lambda i: (i, 0)), pl.BlockSpec((tm, d), lambda i: (i, 0))],
        out_shape=[jax.ShapeDtypeStruct((m, d), F32), jax.ShapeDtypeStruct((m, d), BF16)],
        compiler_params=_cparams(("parallel",)),
        name="add_t_rmsnorm",
    )(x, r_t, w.reshape(1, d))


def _rmsnorm_split_kernel(x_ref, w_ref, o_ref, os_ref):
    y = _rms_body(x_ref[...], w_ref[...]).astype(o_ref.dtype)
    o_ref[...] = y
    n_parts, _, dk = os_ref.shape
    for t in range(n_parts):
        os_ref[t] = y[:, t * dk:(t + 1) * dk]


def rmsnorm_split(x, w, n_parts, tm=256):
    m, d = x.shape
    return pl.pallas_call(
        _rmsnorm_split_kernel,
        grid=(m // tm,),
        in_specs=[pl.BlockSpec((tm, d), lambda i: (i, 0)), pl.BlockSpec((1, d), lambda i: (0, 0))],
        out_specs=[pl.BlockSpec((tm, d), lambda i: (i, 0)),
                   pl.BlockSpec((n_parts, tm, d // n_parts), lambda i: (0, i, 0))],
        out_shape=[jax.ShapeDtypeStruct((m, d), BF16), jax.ShapeDtypeStruct((n_parts, m, d // n_parts), BF16)],
        compiler_params=_cparams(("parallel",)),
        name="rmsnorm_split",
    )(x, w.reshape(1, d))


def _cast_split_kernel(x_ref, o_ref):
    n_parts, _, dk = o_ref.shape
    for t in range(n_parts):
        o_ref[t] = x_ref[:, t * dk:(t + 1) * dk].astype(o_ref.dtype)


def cast_split(x, n_parts, rows=512):
    r, d = x.shape
    rows = min(rows, r)
    return pl.pallas_call(
        _cast_split_kernel,
        grid=(r // rows,),
        in_specs=[pl.BlockSpec((rows, d), lambda i: (i, 0))],
        out_specs=pl.BlockSpec((n_parts, rows, d // n_parts), lambda i: (0, i, 0)),
        out_shape=jax.ShapeDtypeStruct((n_parts, r, d // n_parts), BF16),
        compiler_params=_cparams(("parallel",)),
        name="cast_split",
    )(x)


def _cast_transpose_kernel(x_ref, o_ref):
    o_ref[...] = x_ref[...].T.astype(o_ref.dtype)


def cast_transpose(x, rows=512):
    r, d = x.shape
    rows = min(rows, r)
    return pl.pallas_call(
        _cast_transpose_kernel,
        grid=(r // rows,),
        in_specs=[pl.BlockSpec((rows, d), lambda i: (i, 0))],
        out_specs=pl.BlockSpec((d, rows), lambda i: (0, i)),
        out_shape=jax.ShapeDtypeStruct((d, r), BF16),
        compiler_params=_cparams(("parallel",)),
        name="cast_transpose",
    )(x)


def _mm_kernel(*refs, n_dots, n_extras, epilogue):
    lhs = refs[:n_dots]
    rhs = refs[n_dots:2 * n_dots]
    extras = refs[2 * n_dots:2 * n_dots + n_extras]
    o_ref = refs[2 * n_dots + n_extras]
    dots = [_dot(a[...], b[...].astype(BF16)) for a, b in zip(lhs, rhs)]
    o_ref[...] = epilogue(dots, [e[...] for e in extras]).astype(o_ref.dtype)


def matmul(lhs, rhs, extras=(), extra_col_offsets=None, epilogue=None, out_dtype=F32, tm=1024, tn=512,
           name="matmul"):
    m = lhs[0].shape[0]
    n = rhs[0].shape[1]
    tm = min(tm, m)
    tn = min(tn, n)
    if epilogue is None:
        epilogue = lambda dots, extras: dots[0]
    if extra_col_offsets is None:
        extra_col_offsets = (0,) * len(extras)
    in_specs = [pl.BlockSpec((tm, a.shape[1]), lambda i, j: (i, 0)) for a in lhs]
    in_specs += [pl.BlockSpec((b.shape[0], tn), lambda i, j: (0, j)) for b in rhs]
    for off in extra_col_offsets:
        in_specs.append(pl.BlockSpec((tm, tn), functools.partial(lambda i, j, o: (i, j + o), o=off // tn)))
    return pl.pallas_call(
        functools.partial(_mm_kernel, n_dots=len(lhs), n_extras=len(extras), epilogue=epilogue),
        grid=(m // tm, n // tn),
        in_specs=in_specs,
        out_specs=pl.BlockSpec((tm, tn), lambda i, j: (i, j)),
        out_shape=jax.ShapeDtypeStruct((m, n), out_dtype),
        compiler_params=_cparams(("parallel", "parallel")),
        name=name,
    )(*lhs, *rhs, *extras)


TOK_BETA = 0
TOK_LD = GDN_HEADS
TOK_G = 2 * GDN_HEADS


def _gate_prep_kernel(s_ref, alog_ref, dtb_ref, wa2_ref, ba_ref, tok_ref, la_ref, *, C):
    tm = s_ref.shape[0]
    s = s_ref[...]
    lane = lax.broadcasted_iota(jnp.int32, s.shape, 1)
    ld = jnp.where((lane >= TOK_LD) & (lane < TOK_G), -jnp.exp(alog_ref[...]) * _softplus(s + dtb_ref[...]), 0.0)
    ri = lax.broadcasted_iota(jnp.int32, (tm, tm), 0)
    ci = lax.broadcasted_iota(jnp.int32, (tm, tm), 1)
    tri = ((ri // C == ci // C) & (ci <= ri)).astype(F32)
    g = _dot(tri, ld, precision=HI)
    tok_ref[...] = jnp.where(lane < TOK_LD, _sigmoid(s), ld) + pltpu.roll(g, TOK_G - TOK_LD, 1)
    logit = _bdot(s, wa2_ref[...]) + ba_ref[...]
    la_ref[...] = -_softplus(-logit) / GLA_TAU


def gate_prep(small, a_log, dt_bias, w_a2_pad, b_a, tm=512):
    m = small.shape[0]
    tm = min(tm, m)
    nqk = w_a2_pad.shape[1]
    row = lambda i: (i, 0)
    fixed = lambda i: (0, 0)
    pad = lambda v: jnp.zeros((1, LANES), F32).at[0, TOK_LD:TOK_G].set(v)
    return pl.pallas_call(
        functools.partial(_gate_prep_kernel, C=CHUNK_GDN),
        grid=(m // tm,),
        in_specs=[pl.BlockSpec((tm, LANES), row), pl.BlockSpec((1, LANES), fixed),
                  pl.BlockSpec((1, LANES), fixed), pl.BlockSpec((LANES, nqk), fixed),
                  pl.BlockSpec((1, nqk), fixed)],
        out_specs=[pl.BlockSpec((tm, LANES), row), pl.BlockSpec((tm, nqk), row)],
        out_shape=[jax.ShapeDtypeStruct((m, LANES), F32), jax.ShapeDtypeStruct((m, nqk), F32)],
        compiler_params=_cparams(("parallel",)),
        name="gate_prep",
    )(small, pad(a_log), pad(dt_bias), w_a2_pad, b_a.reshape(1, -1))


def _conv_silu(ref, w_ref, tail_ref, idx, c, C, lanes):
    start = pl.multiple_of(c * C, 8)
    cur = ref[pl.ds(start, C), lanes]
    pstart = pl.multiple_of(jnp.maximum(c * C - 8, 0), 8)
    prev = jnp.where(c > 0, ref[pl.ds(pstart, 8), lanes], tail_ref[idx, :, lanes])
    z = jnp.concatenate([prev, cur], axis=0)
    w = w_ref[:, lanes]
    acc = cur * w[CONV_W - 1:CONV_W, :]
    for s in range(1, CONV_W):
        acc = acc + pltpu.roll(z, s, 0)[8:, :] * w[CONV_W - 1 - s:CONV_W - s, :]
    return _silu(acc)


def _l2norm(x):
    return x * lax.rsqrt(jnp.sum(x * x, axis=-1, keepdims=True) + EPS)


def _gdn_kernel(q_ref, k_ref, v_ref, z_ref, wq_ref, wk_ref, wv_ref, tok_ref, brow_ref, grow_ref, nw_ref, o_ref,
                state_s, tail_s, col_s, q_s, k_s, u_s, w_s, a_s, *, C, G, HG):
    grp = pl.program_id(1)
    sb = pl.program_id(2)
    TS = q_ref.shape[0]
    n_chunks = TS // C
    D = GDN_D
    ri = lax.broadcasted_iota(jnp.int32, (C, C), 0)
    ci = lax.broadcasted_iota(jnp.int32, (C, C), 1)
    causal = ri >= ci
    strict = ri > ci
    eye = (ri == ci).astype(F32)

    @pl.when(sb == 0)
    def _():
        state_s[...] = jnp.zeros_like(state_s)
        tail_s[...] = jnp.zeros_like(tail_s)

    srow = lax.broadcasted_iota(jnp.int32, (LANES, 2 * D), 0)
    scol = lax.broadcasted_iota(jnp.int32, (LANES, 2 * D), 1)
    slab = min(TS, 256)
    for hh in range(HG):
        h = grp * HG + hh
        sel = (srow == jnp.where(scol < D, TOK_BETA + h, TOK_G + h)).astype(F32).astype(BF16)
        for r0 in range(0, TS, slab):
            hi, mid, lo = _split3(tok_ref[r0:r0 + slab, :])
            col_s[hh, r0:r0 + slab, :] = _dot(hi, sel) + _dot(mid, sel) + _dot(lo, sel)

    heads = range(HG)
    head_lanes = [slice(hh * D, (hh + 1) * D) for hh in heads]

    def prep(i, carry):
        items = [(i * G + gg, hh) for gg in range(G) for hh in heads]
        rows = [pl.ds(pl.multiple_of(c * C, 8), C) for c, _ in items]
        k = [_l2norm(_conv_silu(k_ref, wk_ref, tail_s, 1, c, C, head_lanes[hh])) for c, hh in items]
        kb = [x.astype(BF16) for x in k]
        kk = [_dot(x, x, NT) for x in kb]
        beta_row = [jnp.broadcast_to(brow_ref[hh, c], (C, C)) for c, hh in items]
        g_row = [jnp.broadcast_to(grow_ref[hh, c], (C, C)) for c, hh in items]
        decay = [jnp.exp(jnp.where(causal, col_s[hh, r, D:D + C] - gr, -jnp.inf))
                 for (c, hh), r, gr in zip(items, rows, g_row)]
        n_mat = [-jnp.where(strict, x * col_s[hh, r, 0:C] * dc, 0.0)
                 for (c, hh), r, x, dc in zip(items, rows, kk, decay)]
        t_inv = [eye + x for x in n_mat]
        power = n_mat
        v = [_conv_silu(v_ref, wv_ref, tail_s, 2, c, C, head_lanes[hh]) for c, hh in items]
        q = None
        for level in range(int(math.log2(C)) - 1):
            pb = [x.astype(BF16) for x in power]
            power = [_dot(x, x) for x in pb]
            t_inv = [t + _bdot(t, x) for t, x in zip(t_inv, power)]
            if level == 1:
                q = [_l2norm(_conv_silu(q_ref, wq_ref, tail_s, 0, c, C, head_lanes[hh])) * (D ** -0.5)
                     for c, hh in items]
        for n, (c, hh) in enumerate(items):
            r = rows[n]
            q_s[hh, r, :] = q[n]
            k_s[hh, r, :] = k[n]
            u_s[hh, r, :] = _bdot(t_inv[n] * beta_row[n], v[n])
            w_s[hh, r, :] = _bdot(t_inv[n] * (beta_row[n] * jnp.exp(g_row[n])), kb[n])
            a_s[hh, r, :] = _dot(q[n].astype(BF16), kb[n], NT) * decay[n]
        return carry

    lax.fori_loop(0, n_chunks // G, prep, 0)

    def step(c, carry):
        rows = pl.ds(pl.multiple_of(c * C, 8), C)
        g = [col_s[hh, rows, D:] for hh in heads]
        state = [state_s[hh] for hh in heads]
        sb16 = [x.astype(BF16) for x in state]
        v_new = [u_s[hh, rows, :] - _bdot(w_s[hh, rows, :], sb16[hh]) for hh in heads]
        inter = [_bdot(q_s[hh, rows, :], sb16[hh]) for hh in heads]
        for hh in heads:
            g_last = g[hh][C - 1:C, :]
            state_s[hh] = state[hh] * jnp.exp(g_last) + _bdot(k_s[hh, rows, :],
                                                             v_new[hh] * jnp.exp(g_last - g[hh]), TN)
        for hh in heads:
            o = jnp.exp(g[hh]) * inter[hh] + _bdot(a_s[hh, rows, :], v_new[hh])
            o = _rms_body(o, nw_ref[...]) * _silu(z_ref[rows, head_lanes[hh]])
            o_ref[rows, head_lanes[hh]] = o.astype(o_ref.dtype)
        return carry

    lax.fori_loop(0, n_chunks, step, 0)
    tail_s[0] = q_ref[TS - 8:TS, :]
    tail_s[1] = k_ref[TS - 8:TS, :]
    tail_s[2] = v_ref[TS - 8:TS, :]


def gdn(proj, conv_w, tok, brow, grow, norm_w, batch, seq, C=CHUNK_GDN, TS=1024, G=4, HG=4):
    H, D = GDN_HEADS, GDN_D
    TS = min(TS, seq)
    W = HG * D
    col = lambda off: pl.BlockSpec((None, TS, W), functools.partial(lambda b, g, s, o: (b, s, g + o), o=off // W))
    cw = lambda off: pl.BlockSpec((CONV_W, W), functools.partial(lambda b, g, s, o: (0, g + o), o=off // W))
    rows = pl.BlockSpec((None, HG, TS // C, 1, C), lambda b, g, s: (b, g, s, 0, 0))
    blk = lambda lanes: pltpu.VMEM((HG, TS, lanes), F32)
    return pl.pallas_call(
        functools.partial(_gdn_kernel, C=C, G=G, HG=HG),
        grid=(batch, H // HG, seq // TS),
        in_specs=[col(0), col(H * D), col(2 * H * D), col(3 * H * D), cw(0), cw(H * D), cw(2 * H * D),
                  pl.BlockSpec((None, TS, LANES), lambda b, g, s: (b, s, 0)), rows, rows,
                  pl.BlockSpec((1, D), lambda b, g, s: (0, 0))],
        out_specs=pl.BlockSpec((None, TS, W), lambda b, g, s: (b, s, g)),
        out_shape=jax.ShapeDtypeStruct((batch, seq, H * D), BF16),
        scratch_shapes=[pltpu.VMEM((HG, D, D), F32), pltpu.VMEM((3, 8, W), F32), blk(2 * D),
                        blk(D), blk(D), blk(D), blk(D), blk(C)],
        compiler_params=_cparams(("parallel", "parallel", "arbitrary")),
        name="gdn",
    )(proj, proj, proj, proj, conv_w, conv_w, conv_w, tok, brow, grow, norm_w.reshape(1, D))


def _split_bf16(x):
    hi = x.astype(BF16)
    lo = (x - hi.astype(F32)).astype(BF16)
    return hi, lo


def _gla_kernel(q_ref, k_ref, v_ref, r_ref, la_ref, nw_ref, o_ref, *, C, SUB, G):
    n_chunks = q_ref.shape[0] // C
    L = LANES
    DV = GLA_DV
    ri = lax.broadcasted_iota(jnp.int32, (C, C), 0)
    ci = lax.broadcasted_iota(jnp.int32, (C, C), 1)
    tri = (ri >= ci).astype(F32).astype(BF16)
    lane = lax.broadcasted_iota(jnp.int32, (1, L), 1)
    head_of_lane = lane // GLA_DK
    hmask = [(head_of_lane == h).astype(F32) for h in range(2)]
    seg = (lax.broadcasted_iota(jnp.int32, (L, L), 0) // GLA_DK
           == lax.broadcasted_iota(jnp.int32, (L, L), 1) // GLA_DK).astype(F32).astype(BF16)
    sub_row = lax.broadcasted_iota(jnp.int32, (SUB, L), 0)
    lane_s = lax.broadcasted_iota(jnp.int32, (SUB, L), 1)
    row_c = lax.broadcasted_iota(jnp.int32, (C, L), 0)
    stack_mask = (lax.broadcasted_iota(jnp.int32, (2 * C, L), 0) // C
                  == lax.broadcasted_iota(jnp.int32, (2 * C, L), 1) // GLA_DK).astype(F32)
    n_sub = C // SUB
    chunks = range(G)

    def step(i, state_t):
        rows = [pl.ds(pl.multiple_of((i * G + n) * C, 8), C) for n in chunks]
        q = [q_ref[r, :] * (GLA_DK ** -0.5) for r in rows]
        k = [k_ref[r, :] for r in rows]
        b = []
        for r in rows:
            hi, mid, lo = _split3(la_ref[r, :])
            b.append(_dot(tri, hi) + _dot(tri, mid) + _dot(tri, lo))
        blocks = [[jnp.zeros((SUB, L), F32)] + [None] * (n_sub - 1) for _ in chunks]
        for I in range(1, n_sub):
            r0 = I * SUB
            for n in chunks:
                b_ref0 = b[n][r0:r0 + 1, :]
                qt = q[n][r0:r0 + SUB, :] * jnp.exp(b[n][r0:r0 + SUB, :] - b_ref0)
                kt = k[n] * jnp.exp(jnp.where(row_c < r0, b_ref0 - b[n], -jnp.inf))
                kt2 = jnp.concatenate([kt, kt], axis=0) * stack_mask
                blocks[n][I] = _bdot(qt, kt2, NT)
        rsum = []
        for n in chunks:
            ys = []
            for I in range(n_sub):
                r0 = I * SUB
                q_i = q[n][r0:r0 + SUB, :]
                b_i = b[n][r0:r0 + SUB, :]
                for jj in range(SUB):
                    j = r0 + jj
                    ys.append(q_i * k[n][j:j + 1, :]
                              * jnp.exp(jnp.where(sub_row >= jj, b_i - b[n][j:j + 1, :], -jnp.inf)))
            y_hi, y_lo = _split_bf16(jnp.concatenate(ys, axis=0))
            rsum.append(_dot(y_hi, seg) + _dot(y_lo, seg))
        a_pair = []
        for n in chunks:
            for I in range(n_sub):
                acc = blocks[n][I]
                for jj in range(SUB):
                    j = I * SUB + jj
                    acc = acc + jnp.where(lane_s % C == j, rsum[n][j * SUB:(j + 1) * SUB, :], 0.0)
                blocks[n][I] = acc
            a_pair.append(jnp.concatenate(blocks[n], axis=0))
        v = [[v_ref[r, h * DV:(h + 1) * DV] for h in range(2)] for r in rows]
        intra = [[_bdot(a_pair[n] * hmask[h], jnp.concatenate([v[n][h], v[n][h]], axis=0)) for h in range(2)]
                 for n in chunks]
        kv = [[_bdot(v[n][h], k[n] * jnp.exp(b[n][C - 1:C, :] - b[n]) * hmask[h], TN) for h in range(2)]
              for n in chunks]
        state = [state_t[0], state_t[1]]
        for n in chunks:
            qe = q[n] * jnp.exp(b[n])
            outs = []
            for h in range(2):
                o_h = _bdot(qe * hmask[h], state[h], NT) + intra[n][h]
                state[h] = state[h] * jnp.exp(b[n][C - 1:C, :]) + kv[n][h]
                outs.append(_rms_body(o_h, nw_ref[...]) * _silu(r_ref[rows[n], h * DV:(h + 1) * DV]))
            o_ref[rows[n], :] = jnp.concatenate(outs, axis=1).astype(o_ref.dtype)
        return jnp.stack(state)

    lax.fori_loop(0, n_chunks // G, step, jnp.zeros((2, DV, L), F32))


def gla(proj, la, norm_w, batch, seq, q_off, k_off, v_off, r_off, C=CHUNK_GLA, SUB=SUB_GLA, G=4):
    assert 2 * C == LANES and 2 * GLA_DK == LANES
    P = GLA_HEADS // 2
    qk = lambda off: pl.BlockSpec((None, seq, LANES),
                                  functools.partial(lambda b, p, o: (b, 0, p + o), o=off // LANES))
    vr = lambda off: pl.BlockSpec((None, seq, 2 * GLA_DV),
                                  functools.partial(lambda b, p, o: (b, 0, p + o), o=off // (2 * GLA_DV)))
    return pl.pallas_call(
        functools.partial(_gla_kernel, C=C, SUB=SUB, G=G),
        grid=(batch, P),
        in_specs=[qk(q_off), qk(k_off), vr(v_off), vr(r_off),
                  pl.BlockSpec((None, seq, LANES), lambda b, p: (b, 0, p)),
                  pl.BlockSpec((1, GLA_DV), lambda b, p: (0, 0))],
        out_specs=pl.BlockSpec((None, seq, 2 * GLA_DV), lambda b, p: (b, 0, p)),
        out_shape=jax.ShapeDtypeStruct((batch, seq, GLA_HEADS * GLA_DV), BF16),
        compiler_params=_cparams(("parallel", "parallel")),
        name="gla",
    )(proj, proj, proj, proj, la, norm_w.reshape(1, GLA_DV))


def _top_values(s, n):
    vals = []
    cur = s
    for r in range(n):
        m = jnp.max(cur, axis=0, keepdims=True)
        vals.append(m)
        if r + 1 < n:
            cur = jnp.where(cur == m, -jnp.inf, cur)
    return vals


THRESH_SLACK = 4.0 * 2.0 ** -23


def _peer_select_kernel(q_ref, k1_ref, k2_ref, th_ref, b_ref, e1_ref, e2_ref):
    for h in range(PEER_HEADS):
        q1 = q_ref[:, (2 * h) * LANES:(2 * h + 1) * LANES]
        q2 = q_ref[:, (2 * h + 1) * LANES:(2 * h + 2) * LANES]
        s1 = _dot(k1_ref[...], q1, NT, precision=HI)
        s2 = _dot(k2_ref[...], q2, NT, precision=HI)
        v1 = _top_values(s1, TOPK)
        v2 = _top_values(s2, TOPK)
        sums = [v1[a] + v2[b] for a in range(TOPK) for b in range(TOPK) if (a + 1) * (b + 1) <= TOPK]
        sums += [jnp.full_like(v1[0], -jnp.inf)] * (-len(sums) % 8)
        cand = jnp.concatenate(sums, axis=0)
        top = _top_values(cand, TOPK)
        z = top[0] * 0.0
        for t in top:
            z = z + jnp.exp(t - top[0])
        tau = top[TOPK - 1]
        outs = ((th_ref, (tau - s1) - THRESH_SLACK * (jnp.abs(tau) + jnp.abs(s1))), (b_ref, s2),
                (e1_ref, jnp.exp(s1 - v1[0]) / z), (e2_ref, jnp.exp(s2 - v2[0])))
        for ref, val in outs:
            for lt in range(val.shape[1] // LANES):
                ref[lt, h] = val[:, lt * LANES:(lt + 1) * LANES]


def peer_select(q, keys1, keys2, tm=256):
    m = q.shape[0]
    big = pl.BlockSpec((tm // LANES, PEER_HEADS, N_KEYS, LANES), lambda i: (i, 0, 0, 0))
    big_shape = jax.ShapeDtypeStruct((m // LANES, PEER_HEADS, N_KEYS, LANES), F32)
    return pl.pallas_call(
        _peer_select_kernel,
        grid=(m // tm,),
        in_specs=[pl.BlockSpec((tm, q.shape[1]), lambda i: (i, 0)),
                  pl.BlockSpec(keys1.shape, lambda i: (0, 0)), pl.BlockSpec(keys2.shape, lambda i: (0, 0))],
        out_specs=[big, big, big, big],
        out_shape=[big_shape, big_shape, big_shape, big_shape],
        compiler_params=_cparams(("parallel",)),
        name="peer_select",
    )(q, keys1, keys2)


def _peer_main_kernel(hn_ref, u_ref, vt_ref, th_ref, b_ref, e1_ref, e2_ref, o_ref, w_wr, w_rd, s_wr, s_rd, *, GR, nj):
    s = pl.program_id(0)
    n_parts, tm, _ = hn_ref.shape
    n_first = th_ref.shape[2]
    d = o_ref.shape[0]
    rs = d // n_parts
    assert tm // LANES == n_parts

    @pl.when(s == 0)
    def _():
        w_wr[...] = jnp.zeros_like(w_wr)
        s_wr[...] = jnp.zeros_like(s_wr)

    @pl.when(jnp.maximum(s - 2, 0) % nj == 0)
    def _():
        o_ref[...] = jnp.zeros_like(o_ref)

    for lt in range(n_parts):
        ls = slice(lt * LANES, (lt + 1) * LANES)
        w_rd[:, ls] = w_wr[lt]
        s_rd[lt] = s_wr[:, ls]

    def part(t, carry):
        v_rows = pl.ds(pl.multiple_of(t * rs, rs), rs)
        o_ref[v_rows, :] += _dot(vt_ref[v_rows, :], w_rd[...])
        scores = _dot(u_ref[t], hn_ref[t], NT)
        s_wr[...] = jnp.where(t == 0, scores, scores + s_wr[...])
        for r0 in range(0, N_KEYS, GR):
            keys = slice(r0, r0 + GR)
            gate = [jnp.zeros((GR, LANES), F32) for _ in range(n_first)]
            for h in range(PEER_HEADS):
                s2 = b_ref[t, h, keys, :]
                e2 = e2_ref[t, h, keys, :]
                for g in range(n_first):
                    gate[g] = gate[g] + jnp.where(s2 >= th_ref[t, h, g:g + 1, :], e2, 0.0) * e1_ref[t, h, g:g + 1, :]
            for g in range(n_first):
                rows = slice(g * N_KEYS + r0, g * N_KEYS + r0 + GR)
                w_wr[t, rows, :] = (gate[g] * _gelu_tanh(s_rd[t, rows, :])).astype(BF16)
        return carry

    lax.fori_loop(0, n_parts, part, 0)


def peer_main(hn, u_b, vt_b, th, b, e1, e2, tm=512, G1=4):
    _, m, dk = hn.shape
    d = vt_b.shape[0]
    tm = min(tm, m)
    n_lt = tm // LANES
    n_exp = u_b.shape[1]
    te = G1 * N_KEYS
    nj = n_exp // te
    n_items = (m // tm) * nj
    once = pl.Buffered(1)
    item = lambda s, lag: jnp.clip(s - lag, 0, n_items - 1)
    tile = lambda s, lag: item(s, lag) // nj
    chunk = lambda s, lag: item(s, lag) % nj
    tok = pl.BlockSpec((n_lt, PEER_HEADS, N_KEYS, LANES), lambda s: (tile(s, 1), 0, 0, 0), pipeline_mode=once)
    grp = pl.BlockSpec((n_lt, PEER_HEADS, None, G1, LANES), lambda s: (tile(s, 1), 0, chunk(s, 1), 0, 0))
    th = th.reshape(m // LANES, PEER_HEADS, N_KEYS // G1, G1, LANES)
    e1 = e1.reshape(m // LANES, PEER_HEADS, N_KEYS // G1, G1, LANES)
    return pl.pallas_call(
        functools.partial(_peer_main_kernel, GR=32, nj=nj),
        grid=(n_items + 2,),
        in_specs=[pl.BlockSpec((G1, tm, dk), lambda s: (0, tile(s, 0), 0), pipeline_mode=once),
                  pl.BlockSpec((G1, te, dk), lambda s: (0, chunk(s, 0), 0)),
                  pl.BlockSpec((d, te), lambda s: (0, chunk(s, 2))), grp, tok, grp, tok],
        out_specs=pl.BlockSpec((d, tm), lambda s: (0, tile(s, 2))),
        out_shape=jax.ShapeDtypeStruct((d, m), F32),
        scratch_shapes=[pltpu.VMEM((n_lt, te, LANES), BF16), pltpu.VMEM((te, tm), BF16),
                        pltpu.VMEM((te, tm), F32), pltpu.VMEM((n_lt, te, LANES), F32)],
        compiler_params=_cparams(("arbitrary",)),
        name="peer_main",
    )(hn, u_b, vt_b, th, b, e1, e2)


def _merge_epilogue(dots, extras):
    return _sigmoid(extras[0]) * dots[0] + _sigmoid(extras[1]) * dots[1]


def _residual_epilogue(dots, extras):
    return extras[0] + dots[0]


def _ple_epilogue(dots, extras):
    return extras[0] + _sigmoid(dots[0]) * dots[1]


def _layer(x, p_i, batch, seq, norm_mix, w_in, gdn_conv, gdn_a_log, gdn_dt_bias, gdn_norm, gla_w_a2, gla_b_a,
           gla_norm, w_o_gdn, w_o_gla, w_out, norm_ffn, peer_w_query, peer_keys1, peer_keys2, peer_u, peer_v,
           norm_ple, w_ple, w_ple_gate):
    m, d = x.shape
    gdn_w = 4 * GDN_HEADS * GDN_D
    gla_w = 2 * GLA_HEADS * GLA_DK + 2 * GLA_HEADS * GLA_DV
    small0 = gdn_w
    gla0 = small0 + 2 * GDN_HEADS
    a10 = gla0 + gla_w
    gate0 = a10 + GLA_RANK

    w_main = jnp.concatenate([w_in[:, :gdn_w], w_in[:, gla0:a10], w_in[:, gate0:]], axis=1).astype(BF16)
    w_small = jnp.concatenate(
        [w_in[:, small0:gla0], w_in[:, a10:gate0],
         jnp.zeros((d, LANES - 2 * GDN_HEADS - GLA_RANK), w_in.dtype)], axis=1).astype(BF16)
    w_a2_pad = jnp.zeros((LANES, gla_w_a2.shape[1]), F32).at[2 * GDN_HEADS:2 * GDN_HEADS + GLA_RANK].set(gla_w_a2)

    h = rmsnorm(x, norm_mix, BF16)
    proj = matmul([h], [w_main], name="in_proj")
    small = matmul([h], [w_small], name="in_proj_small")
    tok, la = gate_prep(small, gdn_a_log, gdn_dt_bias, w_a2_pad, gla_b_a)

    proj3 = proj.reshape(batch, seq, proj.shape[1])
    to_rows = lambda t: t.reshape(batch, seq, GDN_HEADS).transpose(0, 2, 1).reshape(
        batch, GDN_HEADS, seq // CHUNK_GDN, 1, CHUNK_GDN)
    o_gdn = gdn(proj3, gdn_conv, tok.reshape(batch, seq, LANES), to_rows(tok[:, TOK_BETA:TOK_BETA + GDN_HEADS]),
                to_rows(tok[:, TOK_G:TOK_G + GDN_HEADS]), gdn_norm, batch, seq)
    q_off = gdn_w
    k_off = q_off + GLA_HEADS * GLA_DK
    v_off = k_off + GLA_HEADS * GLA_DK
    r_off = v_off + GLA_HEADS * GLA_DV
    o_gla = gla(proj3, la.reshape(batch, seq, -1), gla_norm, batch, seq, q_off, k_off, v_off, r_off)

    g_off = gdn_w + gla_w
    merged = matmul([o_gdn.reshape(m, -1), o_gla.reshape(m, -1)], [w_o_gdn, w_o_gla],
                    extras=[proj, proj], extra_col_offsets=[g_off, g_off + d], epilogue=_merge_epilogue,
                    out_dtype=BF16, name="merge")
    x = matmul([merged], [w_out], extras=[x], epilogue=_residual_epilogue, name="out_proj")

    hn, hn_split = rmsnorm_split(x, norm_ffn, PEER_SPLIT)
    qp = matmul([hn], [peer_w_query], name="peer_query")
    th, b, e1, e2 = peer_select(qp, peer_keys1, peer_keys2)
    peer_t = peer_main(hn_split, cast_split(peer_u, PEER_SPLIT), cast_transpose(peer_v), th, b, e1, e2,
                       G1=PEER_SPLIT)
    x, n = add_t_rmsnorm(x, peer_t, norm_ple)

    x = matmul([n, p_i.astype(BF16)], [w_ple_gate, w_ple], extras=[x],
               epilogue=_ple_epilogue, name="ple")
    return x


def kernel(x, p, norm_mix, w_in, gdn_conv, gdn_a_log, gdn_dt_bias, gdn_norm, gla_w_a2, gla_b_a, gla_norm,
           w_o_gdn, w_o_gla, w_out, norm_ffn, peer_w_query, peer_keys1, peer_keys2, peer_u, peer_v, norm_ple,
           w_ple, w_ple_gate, norm_final):
    batch, seq, d = x.shape
    depth = p.shape[0]
    xf = x.reshape(batch * seq, d)
    for i in range(depth):
        xf = _layer(xf, p[i].reshape(batch * seq, -1), batch, seq, norm_mix[i], w_in[i], gdn_conv[i],
                    gdn_a_log[i], gdn_dt_bias[i], gdn_norm[i], gla_w_a2[i], gla_b_a[i], gla_norm[i],
                    w_o_gdn[i], w_o_gla[i], w_out[i], norm_ffn[i], peer_w_query[i], peer_keys1[i],
                    peer_keys2[i], peer_u[i], peer_v[i], norm_ple[i], w_ple[i], w_ple_gate[i])
    return rmsnorm(xf, norm_final, F32).reshape(batch, seq, d)
```
